```python
import jax, jax.numpy as jnp
from jax import lax
import numpy as np

D_MODEL = 1024
BATCH = 8
SEQ = 4096
DEPTH = 1

CTX_LEN = 256
GRID_W = 64
HEAD_DIM = 128
N_Q_HEADS = D_MODEL // HEAD_DIM
N_KV_HEADS = N_Q_HEADS // 4
GQA_GROUP = N_Q_HEADS // N_KV_HEADS
ROPE_AXIS_DIM = HEAD_DIM // 2
ROPE_THETA = 10000.0
Q_BLOCK = 128
CONV_DIM = D_MODEL
CONV_WIDTH = 31
PEER_HEADS = 8
PEER_N_KEYS = 128
PEER_EXPERTS = PEER_N_KEYS * PEER_N_KEYS
PEER_TOPK = 16
PEER_KEY_DIM = 256
PEER_HALF = PEER_KEY_DIM // 2
PEER_CHUNK = 128
N_MOD = 6
EPS = 1e-6
Q_W = N_Q_HEADS * HEAD_DIM
KV_W = N_KV_HEADS * HEAD_DIM
IN_SPLITS = (Q_W, Q_W + KV_W, Q_W + 2 * KV_W, Q_W + 2 * KV_W + 2 * CONV_DIM, Q_W + 2 * KV_W + 2 * CONV_DIM + D_MODEL)
IN_W = IN_SPLITS[-1] + D_MODEL

kernel_name = 'hybrid_gqa_conformer_peer_dit_layer'


def rmsnorm(x, g):
    xf = x.astype(jnp.float32)
    y = xf * lax.rsqrt(jnp.mean(xf * xf, axis=-1, keepdims=True) + EPS)
    return (y * g.astype(jnp.float32)).astype(x.dtype)


def layernorm(x, g, b):
    xf = x.astype(jnp.float32)
    mu = jnp.mean(xf, axis=-1, keepdims=True)
    var = jnp.mean(jnp.square(xf - mu), axis=-1, keepdims=True)
    y = (xf - mu) * lax.rsqrt(var + EPS) * g.astype(jnp.float32) + b.astype(jnp.float32)
    return y.astype(x.dtype)


def modulate(h, shift, scale):
    return h * (1 + scale) + shift


def axial_rope_tables(n_tokens, dtype):
    rows = n_tokens // GRID_W
    row = jnp.repeat(jnp.arange(rows, dtype=jnp.float32), GRID_W)
    col = jnp.tile(jnp.arange(GRID_W, dtype=jnp.float32), rows)
    inv = ROPE_THETA ** (-jnp.arange(0, ROPE_AXIS_DIM, 2, dtype=jnp.float32) / ROPE_AXIS_DIM)
    ang_r = row[:, None] * inv
    ang_c = col[:, None] * inv
    return (jnp.cos(ang_r)[:, None, :].astype(dtype), jnp.sin(ang_r)[:, None, :].astype(dtype),
            jnp.cos(ang_c)[:, None, :].astype(dtype), jnp.sin(ang_c)[:, None, :].astype(dtype))


def rotate(xa, cos, sin):
    x1, x2 = jnp.split(xa, 2, axis=-1)
    return jnp.concatenate([x1 * cos - x2 * sin, x1 * sin + x2 * cos], axis=-1)


def apply_rope(x, rope):
    cos_r, sin_r, cos_c, sin_c = rope
    xr, xc = jnp.split(x, 2, axis=-1)
    return jnp.concatenate([rotate(xr, cos_r, sin_r), rotate(xc, cos_c, sin_c)], axis=-1)


def attend(q, k, v):
    B, T = q.shape[0], q.shape[1]
    nb = T // Q_BLOCK
    scale = HEAD_DIM ** -0.5
    qb = q.reshape(B, nb, Q_BLOCK, N_KV_HEADS, GQA_GROUP, HEAD_DIM).transpose(1, 0, 2, 3, 4, 5)

    def block(qi):
        s = jnp.einsum('bqkgd,bskd->bkgqs', qi, k).astype(jnp.float32) * scale
        p = jax.nn.softmax(s, axis=-1).astype(v.dtype)
        return jnp.einsum('bkgqs,bskd->bqkgd', p, v)

    o = lax.map(block, qb)
    return o.transpose(1, 0, 2, 3, 4, 5).reshape(B, T, Q_W)


def conformer_conv(u, lw):
    a, b = jnp.split(u, 2, axis=-1)
    g = a * jax.nn.sigmoid(b)
    pad = CONV_WIDTH // 2
    g = lax.conv_general_dilated(g, lw['conv_dw'][:, None, :], window_strides=(1,), padding=[(pad, pad)],
                                 dimension_numbers=('NWC', 'WIO', 'NWC'), feature_group_count=CONV_DIM)
    g = layernorm(g + lw['conv_b'], lw['conv_ln_g'], lw['conv_ln_b'])
    return jax.nn.silu(g) @ lw['w_conv_o']


def context_kv(h, lw):
    B, T, _ = h.shape
    kv = h @ lw['w_in'][:, Q_W:Q_W + 2 * KV_W]
    k, v = jnp.split(kv, 2, axis=-1)
    k = rmsnorm(k.reshape(B, T, N_KV_HEADS, HEAD_DIM), lw['k_norm_g'])
    return k, v.reshape(B, T, N_KV_HEADS, HEAD_DIM)


def token_mixer(h, lw, rope, k_ctx, v_ctx):
    B, T, _ = h.shape
    q, k, v, u, ga, gc = jnp.split(h @ lw['w_in'], IN_SPLITS, axis=-1)
    q = rmsnorm(q.reshape(B, T, N_Q_HEADS, HEAD_DIM), lw['q_norm_g'])
    k = rmsnorm(k.reshape(B, T, N_KV_HEADS, HEAD_DIM), lw['k_norm_g'])
    v = v.reshape(B, T, N_KV_HEADS, HEAD_DIM)
    if rope is not None:
        q = apply_rope(q, rope)
        k = jnp.concatenate([apply_rope(k, rope), k_ctx], axis=1)
        v = jnp.concatenate([v, v_ctx], axis=1)
    y_att = attend(q, k, v) @ lw['w_attn_o']
    y_conv = conformer_conv(u, lw)
    merged = jax.nn.sigmoid(ga) * y_att + jax.nn.sigmoid(gc) * y_conv
    return merged @ lw['w_out']


def peer_ffn(h, lw):
    B, T, D = h.shape
    wq, keys, u_tab, v_tab = lw['peer_wq'], lw['peer_keys'], lw['peer_u'], lw['peer_v']

    def chunk(xc):
        q = (xc @ wq).reshape(PEER_CHUNK, PEER_HEADS, 2, PEER_HALF)
        s = jnp.einsum('thpd,hpnd->thpn', q, keys)
        s_top, i_top = lax.top_k(s, PEER_TOPK)
        n_cand = PEER_TOPK * PEER_TOPK
        cand_s = (s_top[:, :, 0, :, None] + s_top[:, :, 1, None, :]).reshape(PEER_CHUNK, PEER_HEADS, n_cand)
        cand_i = (i_top[:, :, 0, :, None] * PEER_N_KEYS + i_top[:, :, 1, None, :]).reshape(PEER_CHUNK, PEER_HEADS, n_cand)
        best_s, best_pos = lax.top_k(cand_s, PEER_TOPK)
        eid = jnp.take_along_axis(cand_i, best_pos, axis=-1)
        w = jax.nn.softmax(best_s.astype(jnp.float32), axis=-1).astype(xc.dtype)
        act = jax.nn.gelu(jnp.einsum('thkd,td->thk', jnp.take(u_tab, eid, axis=0), xc), approximate=False)
        return jnp.einsum('thk,thkd->td', w * act, jnp.take(v_tab, eid, axis=0))

    y = lax.map(chunk, h.reshape(-1, PEER_CHUNK, D))
    return y.reshape(B, T, D)


def setup_inputs(seed: int = 0) -> dict:
    key = jax.random.key(seed)
    ks = jax.random.split(key, 24)
    L, D = DEPTH, D_MODEL

    def nrm(k, shape, scale):
        return jax.random.normal(k, shape, jnp.float32) * scale

    return {
        'x': nrm(ks[0], (BATCH, SEQ, D), 1.0),
        'c': nrm(ks[1], (BATCH, D), 1.0),
        'ctx': nrm(ks[2], (BATCH, CTX_LEN, D), 1.0),
        'c_ctx': nrm(ks[3], (D,), 1.0),
        'w_mod': nrm(ks[4], (L, D, N_MOD * D), 0.5 * D ** -0.5),
        'b_mod': nrm(ks[5], (L, N_MOD * D), 0.01),
        'norm1_g': 1.0 + nrm(ks[6], (L, D), 0.1),
        'norm2_g': 1.0 + nrm(ks[7], (L, D), 0.1),
        'w_in': nrm(ks[8], (L, D, IN_W), D ** -0.5),
        'q_norm_g': 1.0 + nrm(ks[9], (L, HEAD_DIM), 0.1),
        'k_norm_g': 1.0 + nrm(ks[10], (L, HEAD_DIM), 0.1),
        'w_attn_o': nrm(ks[11], (L, Q_W, D), Q_W ** -0.5),
        'conv_dw': nrm(ks[12], (L, CONV_WIDTH, CONV_DIM), CONV_WIDTH ** -0.5),
        'conv_b': nrm(ks[13], (L, CONV_DIM), 0.01),
        'conv_ln_g': 1.0 + nrm(ks[14], (L, CONV_DIM), 0.1),
        'conv_ln_b': nrm(ks[15], (L, CONV_DIM), 0.01),
        'w_conv_o': nrm(ks[16], (L, CONV_DIM, D), CONV_DIM ** -0.5),
        'w_out': nrm(ks[17], (L, D, D), D ** -0.5),
        'peer_wq': nrm(ks[18], (L, D, PEER_HEADS * PEER_KEY_DIM), D ** -0.5),
        'peer_keys': nrm(ks[19], (L, PEER_HEADS, 2, PEER_N_KEYS, PEER_HALF), PEER_HALF ** -0.5),
        'peer_u': nrm(ks[20], (L, PEER_EXPERTS, D), D ** -0.5),
        'peer_v': nrm(ks[21], (L, PEER_EXPERTS, D), 0.5),
        'final_norm_g': 1.0 + nrm(ks[22], (D,), 0.1),
    }


def reference(x, c, ctx, c_ctx, w_mod, b_mod, norm1_g, norm2_g, w_in, q_norm_g, k_norm_g,
              w_attn_o, conv_dw, conv_b, conv_ln_g, conv_ln_b, w_conv_o, w_out,
              peer_wq, peer_keys, peer_u, peer_v, final_norm_g):
    S = x.shape[1]
    rope = axial_rope_tables(S, x.dtype)
    xs, cs = x, ctx
    for l in range(DEPTH):
        lw = {'w_in': w_in[l], 'q_norm_g': q_norm_g[l], 'k_norm_g': k_norm_g[l], 'w_attn_o': w_attn_o[l],
              'conv_dw': conv_dw[l], 'conv_b': conv_b[l], 'conv_ln_g': conv_ln_g[l], 'conv_ln_b': conv_ln_b[l],
              'w_conv_o': w_conv_o[l], 'w_out': w_out[l], 'peer_wq': peer_wq[l], 'peer_keys': peer_keys[l],
              'peer_u': peer_u[l], 'peer_v': peer_v[l]}
        last = l == DEPTH - 1
        sh1, sc1, g1, sh2, sc2, g2 = [m[:, None, :] for m in
                                      jnp.split(jax.nn.silu(c) @ w_mod[l] + b_mod[l], N_MOD, axis=-1)]
        csh1, csc1, cg1, csh2, csc2, cg2 = jnp.split(jax.nn.silu(c_ctx) @ w_mod[l] + b_mod[l], N_MOD, axis=-1)
        hc = modulate(rmsnorm(cs, norm1_g[l]), csh1, csc1)
        k_c, v_c = context_kv(hc, lw)
        hx = modulate(rmsnorm(xs, norm1_g[l]), sh1, sc1)
        xs = xs + g1 * token_mixer(hx, lw, rope, k_c, v_c)
        if not last:
            cs = cs + cg1 * token_mixer(hc, lw, None, None, None)
        hx = modulate(rmsnorm(xs, norm2_g[l]), sh2, sc2)
        xs = xs + g2 * peer_ffn(hx, lw)
        if not last:
            hc = modulate(rmsnorm(cs, norm2_g[l]), csh2, csc2)
            cs = cs + cg2 * peer_ffn(hc, lw)
    return rmsnorm(xs, final_norm_g)
```

```python
import functools
import math

import jax
import jax.numpy as jnp
from jax import lax
from jax.experimental import pallas as pl
from jax.experimental.pallas import tpu as pltpu

F32 = jnp.float32
BF16 = jnp.bfloat16

EPS = 1e-6
HEAD_DIM = 128
N_Q_HEADS = 8
N_KV_HEADS = 2
GQA_GROUP = N_Q_HEADS // N_KV_HEADS
GRID_W = 64
ROPE_THETA = 10000.0
ROPE_AXIS_DIM = HEAD_DIM // 2
CONV_WIDTH = 31
CONV_PAD = CONV_WIDTH // 2
CONV_HALO = 16
N_MOD = 6
PEER_HEADS = 8
PEER_N_KEYS = 128
PEER_TOPK = 16
PEER_HALF = 128

VMEM_LIMIT_BYTES = 56 * 1024 * 1024

NT_DIMS = (((1,), (1,)), ((), ()))


def _params(*semantics):
    return pltpu.CompilerParams(dimension_semantics=semantics, vmem_limit_bytes=VMEM_LIMIT_BYTES)


def _rms_modulate(x, g, shift, scale):
    ms = jnp.mean(x * x, axis=-1, keepdims=True)
    y = x * lax.rsqrt(ms + EPS) * g
    return y * (1.0 + scale) + shift


def _mod_kernel(c_ref, w_ref, b_ref, o_ref):
    c = c_ref[...]
    a = c * jax.nn.sigmoid(c)
    o_ref[...] = jnp.dot(a, w_ref[...], preferred_element_type=F32,
                         precision=lax.Precision.HIGHEST) + b_ref[...]


def _mod(cc, w, b):
    m, d = cc.shape
    n = w.shape[1]
    tn = n // 4
    return pl.pallas_call(
        _mod_kernel,
        grid=(n // tn,),
        in_specs=[pl.BlockSpec((m, d), lambda j: (0, 0)),
                  pl.BlockSpec((d, tn), lambda j: (0, j)),
                  pl.BlockSpec((1, tn), lambda j: (0, j))],
        out_specs=pl.BlockSpec((m, tn), lambda j: (0, j)),
        out_shape=jax.ShapeDtypeStruct((m, n), F32),
        compiler_params=_params("arbitrary"),
        name="mod",
    )(cc, w, b)


def _in_proj_kernel(x_ref, g_ref, sh_ref, sc_ref, w_ref, o_ref):
    h = _rms_modulate(x_ref[0], g_ref[...], sh_ref[0], sc_ref[0]).astype(BF16)
    o_ref[0] = jnp.dot(h, w_ref[...], preferred_element_type=F32).astype(o_ref.dtype)


def _in_proj(x, g, sh, sc, w, tm):
    b, s, d = x.shape
    n = w.shape[1]
    return pl.pallas_call(
        _in_proj_kernel,
        grid=(b, s // tm),
        in_specs=[pl.BlockSpec((1, tm, d), lambda bi, i: (bi, i, 0)),
                  pl.BlockSpec((1, d), lambda bi, i: (0, 0)),
                  pl.BlockSpec((1, 1, d), lambda bi, i: (bi, 0, 0)),
                  pl.BlockSpec((1, 1, d), lambda bi, i: (bi, 0, 0)),
                  pl.BlockSpec((d, n), lambda bi, i: (0, 0))],
        out_specs=pl.BlockSpec((1, tm, n), lambda bi, i: (bi, i, 0)),
        out_shape=jax.ShapeDtypeStruct((b, s, n), BF16),
        compiler_params=_params("parallel", "parallel"),
        name="in_proj",
    )(x, g, sh, sc, w)


def _head_prep_kernel(x_ref, cos_ref, sin_ref, g_ref, o_ref, *, n_heads, scale):
    cos = cos_ref[...]
    sin = sin_ref[...]
    g = g_ref[...] * scale
    lane = lax.broadcasted_iota(jnp.int32, cos.shape, 1)
    first = (lane % ROPE_AXIS_DIM) < (ROPE_AXIS_DIM // 2)
    for hd in range(n_heads):
        x = x_ref[0, :, hd * HEAD_DIM:(hd + 1) * HEAD_DIM].astype(F32)
        ms = jnp.mean(x * x, axis=-1, keepdims=True)
        y = x * lax.rsqrt(ms + EPS) * g
        partner = jnp.where(first, pltpu.roll(y, HEAD_DIM - ROPE_AXIS_DIM // 2, 1),
                            pltpu.roll(y, ROPE_AXIS_DIM // 2, 1))
        o_ref[0, :, hd * HEAD_DIM:(hd + 1) * HEAD_DIM] = (y * cos + partner * sin).astype(o_ref.dtype)


def _head_prep(x, col_block, n_heads, cos, sin, g, scale, tm):
    b, s, _ = x.shape
    w = n_heads * HEAD_DIM
    return pl.pallas_call(
        functools.partial(_head_prep_kernel, n_heads=n_heads, scale=scale),
        grid=(b, s // tm),
        in_specs=[pl.BlockSpec((1, tm, w), lambda bi, i: (bi, i, col_block)),
                  pl.BlockSpec((tm, HEAD_DIM), lambda bi, i: (i, 0)),
                  pl.BlockSpec((tm, HEAD_DIM), lambda bi, i: (i, 0)),
                  pl.BlockSpec((1, HEAD_DIM), lambda bi, i: (0, 0))],
        out_specs=pl.BlockSpec((1, tm, w), lambda bi, i: (bi, i, 0)),
        out_shape=jax.ShapeDtypeStruct((b, s, w), BF16),
        compiler_params=_params("parallel", "parallel"),
        name="head_prep",
    )(x, cos, sin, g)


def _rope_tables(s):
    t = jnp.arange(s, dtype=jnp.int32)
    row = (t // GRID_W).astype(F32)
    col = (t % GRID_W).astype(F32)
    inv = ROPE_THETA ** (-jnp.arange(0, ROPE_AXIS_DIM, 2, dtype=F32) / ROPE_AXIS_DIM)
    ang_r = row[:, None] * inv
    ang_c = col[:, None] * inv
    cos = jnp.concatenate([jnp.cos(ang_r), jnp.cos(ang_r), jnp.cos(ang_c), jnp.cos(ang_c)], axis=-1)
    sin = jnp.concatenate([-jnp.sin(ang_r), jnp.sin(ang_r), -jnp.sin(ang_c), jnp.sin(ang_c)], axis=-1)
    return cos, sin


def _attention_kernel(q_ref, k_ref, v_ref, o_ref):
    k = k_ref[0]
    v = v_ref[0]
    for g in range(GQA_GROUP):
        q = q_ref[0, :, g * HEAD_DIM:(g + 1) * HEAD_DIM]
        s = lax.dot_general(q, k, NT_DIMS, preferred_element_type=F32)
        m = jnp.max(s, axis=-1, keepdims=True)
        p = jnp.exp(s - m)
        l = jnp.sum(p, axis=-1, keepdims=True)
        o = jnp.dot(p.astype(BF16), v, preferred_element_type=F32)
        o_ref[0, :, g * HEAD_DIM:(g + 1) * HEAD_DIM] = (o / l).astype(o_ref.dtype)


def _attention(q, k, v, tq):
    b, s, _ = q.shape
    skv = k.shape[1]
    gw = GQA_GROUP * HEAD_DIM
    return pl.pallas_call(
        _attention_kernel,
        grid=(b, N_KV_HEADS, s // tq),
        in_specs=[pl.BlockSpec((1, tq, gw), lambda bi, kh, i: (bi, i, kh)),
                  pl.BlockSpec((1, skv, HEAD_DIM), lambda bi, kh, i: (bi, 0, kh)),
                  pl.BlockSpec((1, skv, HEAD_DIM), lambda bi, kh, i: (bi, 0, kh))],
        out_specs=pl.BlockSpec((1, tq, gw), lambda bi, kh, i: (bi, i, kh)),
        out_shape=jax.ShapeDtypeStruct((b, s, N_Q_HEADS * HEAD_DIM), BF16),
        compiler_params=_params("parallel", "parallel", "parallel"),
        name="attention",
    )(q, k, v)


def _conv_kernel(ap_ref, a_ref, an_ref, bp_ref, b_ref, bn_ref, w_ref, cb_ref, lg_ref, lb_ref,
                 o_ref, gext_ref, acc_ref, *, ts, rows, strip):
    i = pl.program_id(1)
    last = pl.num_programs(1) - 1

    def glu(a, b):
        return a.astype(F32) * jax.nn.sigmoid(b.astype(F32))

    gext_ref[0:CONV_HALO, :] = jnp.where(i > 0, glu(ap_ref[0], bp_ref[0]), 0.0)
    gext_ref[CONV_HALO:CONV_HALO + ts, :] = glu(a_ref[0], b_ref[0])
    gext_ref[CONV_HALO + ts:, :] = jnp.where(i < last, glu(an_ref[0], bn_ref[0]), 0.0)

    c = a_ref.shape[-1]
    off = CONV_HALO - CONV_PAD

    def chunk(r, carry):
        r0 = pl.multiple_of(r * rows, rows)
        for cs in range(c // strip):
            lanes = slice(cs * strip, (cs + 1) * strip)
            win = gext_ref[pl.ds(r0, rows + 2 * CONV_HALO), lanes]
            acc = jnp.zeros((rows, strip), F32)
            for k in range(CONV_WIDTH):
                acc = acc + win[off + k:off + k + rows, :] * w_ref[k:k + 1, lanes]
            acc_ref[pl.ds(r0, rows), lanes] = acc
        return carry

    lax.fori_loop(0, ts // rows, chunk, 0)

    y = acc_ref[...] + cb_ref[...]
    mu = jnp.mean(y, axis=-1, keepdims=True)
    yc = y - mu
    var = jnp.mean(yc * yc, axis=-1, keepdims=True)
    z = yc * lax.rsqrt(var + EPS) * lg_ref[...] + lb_ref[...]
    o_ref[0] = (z * jax.nn.sigmoid(z)).astype(o_ref.dtype)


def _conv(proj, a_blk, b_blk, w, cb, lg, lb, ts):
    b, s, _ = proj.shape
    c = w.shape[1]
    hb = ts // CONV_HALO
    n_halo = s // CONV_HALO

    def main(col):
        return pl.BlockSpec((1, ts, c), lambda bi, i: (bi, i, col))

    def prev(col):
        return pl.BlockSpec((1, CONV_HALO, c), lambda bi, i: (bi, jnp.maximum(i * hb - 1, 0), col))

    def nxt(col):
        return pl.BlockSpec((1, CONV_HALO, c), lambda bi, i: (bi, jnp.minimum((i + 1) * hb, n_halo - 1), col))

    vec = pl.BlockSpec((1, c), lambda bi, i: (0, 0))
    return pl.pallas_call(
        functools.partial(_conv_kernel, ts=ts, rows=64, strip=256),
        grid=(b, s // ts),
        in_specs=[prev(a_blk), main(a_blk), nxt(a_blk), prev(b_blk), main(b_blk), nxt(b_blk),
                  pl.BlockSpec((CONV_WIDTH, c), lambda bi, i: (0, 0)), vec, vec, vec],
        out_specs=pl.BlockSpec((1, ts, c), lambda bi, i: (bi, i, 0)),
        out_shape=jax.ShapeDtypeStruct((b, s, c), BF16),
        scratch_shapes=[pltpu.VMEM((ts + 2 * CONV_HALO, c), F32), pltpu.VMEM((ts, c), F32)],
        compiler_params=_params("parallel", "parallel"),
        name="conv",
    )(proj, proj, proj, proj, proj, proj, w, cb, lg, lb)


def _merge_kernel(att_ref, cv_ref, ga_ref, gc_ref, x_ref, g1_ref, wa_ref, wc_ref, wo_ref, o_ref):
    y_att = jnp.dot(att_ref[0], wa_ref[...], preferred_element_type=F32)
    y_conv = jnp.dot(cv_ref[0], wc_ref[...], preferred_element_type=F32)
    merged = (jax.nn.sigmoid(ga_ref[0].astype(F32)) * y_att
              + jax.nn.sigmoid(gc_ref[0].astype(F32)) * y_conv)
    y = jnp.dot(merged.astype(BF16), wo_ref[...], preferred_element_type=F32)
    o_ref[0] = x_ref[0] + g1_ref[0] * y


def _merge(att, cv, proj, ga_blk, gc_blk, x, g1, wa, wc, wo, tm):
    b, s, d = x.shape
    tile = lambda col: pl.BlockSpec((1, tm, d), lambda bi, i: (bi, i, col))
    wspec = pl.BlockSpec((d, d), lambda bi, i: (0, 0))
    return pl.pallas_call(
        _merge_kernel,
        grid=(b, s // tm),
        in_specs=[tile(0), tile(0), tile(ga_blk), tile(gc_blk), tile(0),
                  pl.BlockSpec((1, 1, d), lambda bi, i: (bi, 0, 0)), wspec, wspec, wspec],
        out_specs=tile(0),
        out_shape=jax.ShapeDtypeStruct((b, s, d), F32),
        compiler_params=_params("parallel", "parallel"),
        name="merge",
    )(att, cv, proj, proj, x, g1, wa, wc, wo)


N_EXTRACT = PEER_TOPK + 1
N_CAND = sum(N_EXTRACT // (k + 1) for k in range(N_EXTRACT))
N_CAND_ROWS = -(-N_CAND // 8) * 8


def _top_values(s, n):
    out = []
    for _ in range(n):
        m = jnp.max(s, axis=0, keepdims=True)
        out.append(m)
        s = jnp.where(s == m, -jnp.inf, s)
    return out


def _peer_sel_kernel(xs_ref, g_ref, sh_ref, sc_ref, wqt_ref, keys_ref,
                     hx_ref, thr_ref, c_ref, s1_ref, e1_ref, q_ref, cand_ref):
    hb = _rms_modulate(xs_ref[0], g_ref[...], sh_ref[0], sc_ref[0]).astype(BF16)
    hx_ref[0] = hb
    q_ref[...] = lax.dot_general(wqt_ref[...], hb, NT_DIMS, preferred_element_type=F32).astype(BF16)

    def head(hd, carry):
        scores, tops = [], []
        for p in range(2):
            hp = hd * 2 + p
            q = q_ref[pl.ds(pl.multiple_of(hp * PEER_HALF, PEER_HALF), PEER_HALF), :]
            s = jnp.dot(keys_ref[hp], q, preferred_element_type=F32)
            scores.append(s)
            tops.append(_top_values(s, N_EXTRACT))
        a0, a1 = tops
        cands = [a0[k] + a1[l] for k in range(N_EXTRACT) for l in range(N_EXTRACT)
                 if (k + 1) * (l + 1) <= N_EXTRACT]
        cand_ref[...] = jnp.full(cand_ref.shape, -jnp.inf, F32)
        for r, row in enumerate(cands):
            cand_ref[r:r + 1, :] = row
        best = _top_values(cand_ref[...], N_EXTRACT)
        tau = 0.5 * (best[PEER_TOPK - 1] + best[PEER_TOPK])
        z = jnp.ones_like(tau)
        for r in range(1, PEER_TOPK):
            z = z + jnp.exp(best[r] - best[0])
        s0, s1 = scores
        thr_ref[0, hd] = tau - s0
        c_ref[0, hd] = jnp.exp(s0 - a0[0]) / z
        s1_ref[0, hd] = s1
        e1_ref[0, hd] = jnp.exp(s1 - a1[0])
        return carry

    lax.fori_loop(0, PEER_HEADS, head, 0)


def _peer_sel(xs, g, sh, sc, wqt, keys, t):
    b, s, d = xs.shape
    nq = wqt.shape[0]
    sel = pl.BlockSpec((1, PEER_HEADS, PEER_N_KEYS, t), lambda bi, i: (bi, 0, 0, i))
    sel_shape = jax.ShapeDtypeStruct((b, PEER_HEADS, PEER_N_KEYS, s), F32)
    return pl.pallas_call(
        _peer_sel_kernel,
        grid=(b, s // t),
        in_specs=[pl.BlockSpec((1, t, d), lambda bi, i: (bi, i, 0)),
                  pl.BlockSpec((1, d), lambda bi, i: (0, 0)),
                  pl.BlockSpec((1, 1, d), lambda bi, i: (bi, 0, 0)),
                  pl.BlockSpec((1, 1, d), lambda bi, i: (bi, 0, 0)),
                  pl.BlockSpec((nq, d), lambda bi, i: (0, 0)),
                  pl.BlockSpec((2 * PEER_HEADS, PEER_N_KEYS, PEER_HALF), lambda bi, i: (0, 0, 0))],
        out_specs=[pl.BlockSpec((1, t, d), lambda bi, i: (bi, i, 0)), sel, sel, sel, sel],
        out_shape=[jax.ShapeDtypeStruct((b, s, d), BF16), sel_shape, sel_shape, sel_shape, sel_shape],
        scratch_shapes=[pltpu.VMEM((nq, t), BF16), pltpu.VMEM((N_CAND_ROWS, t), F32)],
        compiler_params=_params("parallel", "parallel"),
        name="peer_sel",
    )(xs, g, sh, sc, wqt, keys)


def _peer_ffn_kernel(hx_ref, u_ref, vt_ref, thr_ref, c_ref, s1_ref, e1_ref, xs_ref, g2_ref, fg_ref,
                     o_ref, acc_ref, p_ref, *, ib, t):
    ec = pl.program_id(2)

    @pl.when(ec == 0)
    def _():
        acc_ref[...] = jnp.zeros_like(acc_ref)

    act = lax.dot_general(u_ref[...], hx_ref[0], NT_DIMS, preferred_element_type=F32)
    for blk in range(ib):
        rows = slice(blk * PEER_N_KEYS, (blk + 1) * PEER_N_KEYS)
        for tg in range(t // 128):
            lanes = slice(tg * 128, (tg + 1) * 128)
            w = jnp.zeros((PEER_N_KEYS, 128), F32)
            for hd in range(PEER_HEADS):
                thr = thr_ref[0, hd, blk:blk + 1, lanes]
                c = c_ref[0, hd, blk:blk + 1, lanes]
                w = w + jnp.where(s1_ref[0, hd, :, lanes] >= thr, e1_ref[0, hd, :, lanes], 0.0) * c
            a = act[rows, lanes]
            gelu = 0.5 * a * (1.0 + lax.erf(a * (1.0 / math.sqrt(2.0))))
            p_ref[rows, lanes] = (w * gelu).astype(BF16)
    acc_ref[...] += jnp.dot(vt_ref[...], p_ref[...], preferred_element_type=F32)

    @pl.when(ec == pl.num_programs(2) - 1)
    def _():
        x = xs_ref[0] + g2_ref[0] * acc_ref[...].T
        ms = jnp.mean(x * x, axis=-1, keepdims=True)
        o_ref[0] = x * lax.rsqrt(ms + EPS) * fg_ref[...]


def _peer_ffn(hx, u, vt, thr, c, s1, e1, xs, g2, fg, t, ib):
    b, s, d = xs.shape
    n_exp = u.shape[0]
    ecw = ib * PEER_N_KEYS
    row_sel = pl.BlockSpec((1, PEER_HEADS, ib, t), lambda bi, i, e: (bi, 0, e, i))
    all_sel = pl.BlockSpec((1, PEER_HEADS, PEER_N_KEYS, t), lambda bi, i, e: (bi, 0, 0, i))
    return pl.pallas_call(
        functools.partial(_peer_ffn_kernel, ib=ib, t=t),
        grid=(b, s // t, n_exp // ecw),
        in_specs=[pl.BlockSpec((1, t, d), lambda bi, i, e: (bi, i, 0)),
                  pl.BlockSpec((ecw, d), lambda bi, i, e: (e, 0)),
                  pl.BlockSpec((d, ecw), lambda bi, i, e: (0, e)),
                  row_sel, row_sel, all_sel, all_sel,
                  pl.BlockSpec((1, t, d), lambda bi, i, e: (bi, i, 0)),
                  pl.BlockSpec((1, 1, d), lambda bi, i, e: (bi, 0, 0)),
                  pl.BlockSpec((1, d), lambda bi, i, e: (0, 0))],
        out_specs=pl.BlockSpec((1, t, d), lambda bi, i, e: (bi, i, 0)),
        out_shape=jax.ShapeDtypeStruct((b, s, d), F32),
        scratch_shapes=[pltpu.VMEM((d, t), F32), pltpu.VMEM((ecw, t), BF16)],
        compiler_params=_params("parallel", "parallel", "arbitrary"),
        name="peer_ffn",
    )(hx, u, vt, thr, c, s1, e1, xs, g2, fg)


def kernel(x, c, ctx, c_ctx, w_mod, b_mod, norm1_g, norm2_g, w_in, q_norm_g, k_norm_g, w_attn_o, conv_dw,
           conv_b, conv_ln_g, conv_ln_b, w_conv_o, w_out, peer_wq, peer_keys, peer_u, peer_v, final_norm_g):
    b, s, d = x.shape
    n_ctx = ctx.shape[1]
    assert w_mod.shape[0] == 1, "single-layer stack"
    qw = N_Q_HEADS * HEAD_DIM
    kvw = N_KV_HEADS * HEAD_DIM
    assert d == qw and conv_dw.shape[2] == d

    m_pad = -(-(b + 1) // 8) * 8
    cc = jnp.zeros((m_pad, d), F32).at[:b].set(c).at[b].set(c_ctx)
    mod = _mod(cc, w_mod[0], b_mod[0][None])
    sh1, sc1, g1, sh2, sc2, g2 = [mod[:b, None, j * d:(j + 1) * d] for j in range(N_MOD)]
    csh1, csc1 = [jnp.broadcast_to(mod[b:b + 1, None, j * d:(j + 1) * d], (b, 1, d)) for j in range(2)]

    wi = w_in[0]
    o_k, o_v, o_u, o_ga, o_gc = qw, qw + kvw, qw + 2 * kvw, qw + 2 * kvw + 2 * d, qw + 2 * kvw + 3 * d
    w_cat = jnp.concatenate([wi[:, :qw], wi[:, o_u:o_u + 2 * d], wi[:, o_ga:o_gc], wi[:, o_gc:],
                             wi[:, o_k:o_v], wi[:, o_v:o_u]], axis=1).astype(BF16)
    a_blk, b_blk, ga_blk, gc_blk = 1, 2, 3, 4
    kv_col = 5 * d

    n1 = norm1_g[0][None]
    tm = min(512, s)
    proj = _in_proj(x, n1, sh1, sc1, w_cat, tm)
    proj_c = _in_proj(ctx, n1, csh1, csc1, w_cat[:, kv_col:], min(256, n_ctx))

    cos, sin = _rope_tables(s)
    qg, kg = q_norm_g[0][None], k_norm_g[0][None]
    q = _head_prep(proj, 0, N_Q_HEADS, cos, sin, qg, HEAD_DIM ** -0.5, tm)
    k_lat = _head_prep(proj, kv_col // kvw, N_KV_HEADS, cos, sin, kg, 1.0, tm)
    k_ctx = _head_prep(proj_c, 0, N_KV_HEADS, jnp.ones((n_ctx, HEAD_DIM), F32),
                       jnp.zeros((n_ctx, HEAD_DIM), F32), kg, 1.0, min(256, n_ctx))
    k_all = jnp.concatenate([k_lat, k_ctx], axis=1)
    v_all = jnp.concatenate([proj[:, :, kv_col + kvw:], proj_c[:, :, kvw:]], axis=1)
    att = _attention(q, k_all, v_all, min(256, s))

    cv = _conv(proj, a_blk, b_blk, conv_dw[0], conv_b[0][None], conv_ln_g[0][None], conv_ln_b[0][None], tm)
    xs = _merge(att, cv, proj, ga_blk, gc_blk, x, g1, w_attn_o[0].astype(BF16), w_conv_o[0].astype(BF16),
                w_out[0].astype(BF16), tm)

    tp = min(512, s)
    wqt = peer_wq[0].T.astype(BF16)
    keys = peer_keys[0].reshape(2 * PEER_HEADS, PEER_N_KEYS, PEER_HALF).astype(BF16)
    hx, thr, cw, s1, e1 = _peer_sel(xs, norm2_g[0][None], sh2, sc2, wqt, keys, tp)
    u_b = peer_u[0].astype(BF16)
    vt_b = peer_v[0].T.astype(BF16)
    return _peer_ffn(hx, u_b, vt_b, thr, cw, s1, e1, xs, g2, final_norm_g[None], tp, 8)
```

```python
import functools
import math

import jax
import jax.numpy as jnp
from jax import lax
from jax.experimental import pallas as pl
from jax.experimental.pallas import tpu as pltpu

F32 = jnp.float32
BF16 = jnp.bfloat16

EPS = 1e-6
HEAD_DIM = 128
N_Q_HEADS = 8
N_KV_HEADS = 2
GQA_GROUP = N_Q_HEADS // N_KV_HEADS
GRID_W = 64
ROPE_THETA = 10000.0
ROPE_AXIS_DIM = HEAD_DIM // 2
CONV_WIDTH = 31
CONV_PAD = CONV_WIDTH // 2
CONV_HALO = 16
N_MOD = 6
PEER_HEADS = 8
PEER_N_KEYS = 128
PEER_TOPK = 16
PEER_HALF = 128

VMEM_LIMIT_BYTES = 56 * 1024 * 1024

NT_DIMS = (((1,), (1,)), ((), ()))


def _params(*semantics):
    return pltpu.CompilerParams(dimension_semantics=semantics, vmem_limit_bytes=VMEM_LIMIT_BYTES)


def _rms_modulate(x, g, shift, scale):
    ms = jnp.mean(x * x, axis=-1, keepdims=True)
    y = x * lax.rsqrt(ms + EPS) * g
    return y * (1.0 + scale) + shift


def _mod_kernel(c_ref, w_ref, b_ref, o_ref):
    c = c_ref[...]
    a = c * jax.nn.sigmoid(c)
    o_ref[...] = jnp.dot(a, w_ref[...], preferred_element_type=F32,
                         precision=lax.Precision.HIGHEST) + b_ref[...]


def _mod(cc, w, b):
    m, d = cc.shape
    n = w.shape[1]
    tn = n // 4
    return pl.pallas_call(
        _mod_kernel,
        grid=(n // tn,),
        in_specs=[pl.BlockSpec((m, d), lambda j: (0, 0)),
                  pl.BlockSpec((d, tn), lambda j: (0, j)),
                  pl.BlockSpec((1, tn), lambda j: (0, j))],
        out_specs=pl.BlockSpec((m, tn), lambda j: (0, j)),
        out_shape=jax.ShapeDtypeStruct((m, n), F32),
        compiler_params=_params("arbitrary"),
        name="mod",
    )(cc, w, b)


def _in_proj_kernel(x_ref, g_ref, sh_ref, sc_ref, w_ref, o_ref):
    h = _rms_modulate(x_ref[0], g_ref[...], sh_ref[0], sc_ref[0]).astype(BF16)
    o_ref[0] = jnp.dot(h, w_ref[...], preferred_element_type=F32).astype(o_ref.dtype)


def _in_proj(x, g, sh, sc, w, tm):
    b, s, d = x.shape
    n = w.shape[1]
    return pl.pallas_call(
        _in_proj_kernel,
        grid=(b, s // tm),
        in_specs=[pl.BlockSpec((1, tm, d), lambda bi, i: (bi, i, 0)),
                  pl.BlockSpec((1, d), lambda bi, i: (0, 0)),
                  pl.BlockSpec((1, 1, d), lambda bi, i: (bi, 0, 0)),
                  pl.BlockSpec((1, 1, d), lambda bi, i: (bi, 0, 0)),
                  pl.BlockSpec((d, n), lambda bi, i: (0, 0))],
        out_specs=pl.BlockSpec((1, tm, n), lambda bi, i: (bi, i, 0)),
        out_shape=jax.ShapeDtypeStruct((b, s, n), BF16),
        compiler_params=_params("parallel", "parallel"),
        name="in_proj",
    )(x, g, sh, sc, w)


def _head_prep_kernel(x_ref, cos_ref, sin_ref, g_ref, o_ref, *, n_heads, scale):
    cos = cos_ref[...]
    sin = sin_ref[...]
    g = g_ref[...] * scale
    lane = lax.broadcasted_iota(jnp.int32, cos.shape, 1)
    first = (lane % ROPE_AXIS_DIM) < (ROPE_AXIS_DIM // 2)
    for hd in range(n_heads):
        x = x_ref[0, :, hd * HEAD_DIM:(hd + 1) * HEAD_DIM].astype(F32)
        ms = jnp.mean(x * x, axis=-1, keepdims=True)
        y = x * lax.rsqrt(ms + EPS) * g
        partner = jnp.where(first, pltpu.roll(y, HEAD_DIM - ROPE_AXIS_DIM // 2, 1),
                            pltpu.roll(y, ROPE_AXIS_DIM // 2, 1))
        o_ref[0, :, hd * HEAD_DIM:(hd + 1) * HEAD_DIM] = (y * cos + partner * sin).astype(o_ref.dtype)


def _head_prep(x, col_block, n_heads, cos, sin, g, scale, tm):
    b, s, _ = x.shape
    w = n_heads * HEAD_DIM
    return pl.pallas_call(
        functools.partial(_head_prep_kernel, n_heads=n_heads, scale=scale),
        grid=(b, s // tm),
        in_specs=[pl.BlockSpec((1, tm, w), lambda bi, i: (bi, i, col_block)),
                  pl.BlockSpec((tm, HEAD_DIM), lambda bi, i: (i, 0)),
                  pl.BlockSpec((tm, HEAD_DIM), lambda bi, i: (i, 0)),
                  pl.BlockSpec((1, HEAD_DIM), lambda bi, i: (0, 0))],
        out_specs=pl.BlockSpec((1, tm, w), lambda bi, i: (bi, i, 0)),
        out_shape=jax.ShapeDtypeStruct((b, s, w), BF16),
        compiler_params=_params("parallel", "parallel"),
        name="head_prep",
    )(x, cos, sin, g)


def _rope_tables(s):
    t = jnp.arange(s, dtype=jnp.int32)
    row = (t // GRID_W).astype(F32)
    col = (t % GRID_W).astype(F32)
    inv = ROPE_THETA ** (-jnp.arange(0, ROPE_AXIS_DIM, 2, dtype=F32) / ROPE_AXIS_DIM)
    ang_r = row[:, None] * inv
    ang_c = col[:, None] * inv
    cos = jnp.concatenate([jnp.cos(ang_r), jnp.cos(ang_r), jnp.cos(ang_c), jnp.cos(ang_c)], axis=-1)
    sin = jnp.concatenate([-jnp.sin(ang_r), jnp.sin(ang_r), -jnp.sin(ang_c), jnp.sin(ang_c)], axis=-1)
    return cos, sin


def _attention_kernel(q_ref, k_ref, v_ref, o_ref):
    k = k_ref[0]
    v = v_ref[0]
    for g in range(GQA_GROUP):
        q = q_ref[0, :, g * HEAD_DIM:(g + 1) * HEAD_DIM]
        s = lax.dot_general(q, k, NT_DIMS, preferred_element_type=F32)
        m = jnp.max(s, axis=-1, keepdims=True)
        p = jnp.exp(s - m)
        l = jnp.sum(p, axis=-1, keepdims=True)
        o = jnp.dot(p.astype(BF16), v, preferred_element_type=F32)
        o_ref[0, :, g * HEAD_DIM:(g + 1) * HEAD_DIM] = (o / l).astype(o_ref.dtype)


def _attention(q, k, v, tq):
    b, s, _ = q.shape
    skv = k.shape[1]
    gw = GQA_GROUP * HEAD_DIM
    return pl.pallas_call(
        _attention_kernel,
        grid=(b, N_KV_HEADS, s // tq),
        in_specs=[pl.BlockSpec((1, tq, gw), lambda bi, kh, i: (bi, i, kh)),
                  pl.BlockSpec((1, skv, HEAD_DIM), lambda bi, kh, i: (bi, 0, kh)),
                  pl.BlockSpec((1, skv, HEAD_DIM), lambda bi, kh, i: (bi, 0, kh))],
        out_specs=pl.BlockSpec((1, tq, gw), lambda bi, kh, i: (bi, i, kh)),
        out_shape=jax.ShapeDtypeStruct((b, s, N_Q_HEADS * HEAD_DIM), BF16),
        compiler_params=_params("parallel", "parallel", "parallel"),
        name="attention",
    )(q, k, v)


def _conv_kernel(ap_ref, a_ref, an_ref, bp_ref, b_ref, bn_ref, w_ref, cb_ref, lg_ref, lb_ref,
                 o_ref, gext_ref, acc_ref, sh_ref, *, ts, rows, strip):
    i = pl.program_id(1)
    last = pl.num_programs(1) - 1

    def glu(a, b):
        return a.astype(F32) * jax.nn.sigmoid(b.astype(F32))

    gext_ref[0:CONV_HALO, :] = jnp.where(i > 0, glu(ap_ref[0], bp_ref[0]), 0.0)
    gext_ref[CONV_HALO:CONV_HALO + ts, :] = glu(a_ref[0], b_ref[0])
    gext_ref[CONV_HALO + ts:, :] = jnp.where(i < last, glu(an_ref[0], bn_ref[0]), 0.0)

    c = a_ref.shape[-1]
    off = CONV_HALO - CONV_PAD

    def chunk(r, carry):
        r0 = pl.multiple_of(r * rows, rows)
        for cs in range(c // strip):
            lanes = slice(cs * strip, (cs + 1) * strip)
            win = gext_ref[pl.ds(r0, rows + 2 * CONV_HALO), lanes]
            for r in range(8):
                sh_ref[r] = win[r:r + rows + 2 * CONV_HALO - 8, :]
            acc = jnp.zeros((rows, strip), F32)
            for k in range(CONV_WIDTH):
                q, r = divmod(off + k, 8)
                acc = acc + sh_ref[r, 8 * q:8 * q + rows, :] * w_ref[k:k + 1, lanes]
            acc_ref[pl.ds(r0, rows), lanes] = acc
        return carry

    lax.fori_loop(0, ts // rows, chunk, 0)

    y = acc_ref[...] + cb_ref[...]
    mu = jnp.mean(y, axis=-1, keepdims=True)
    yc = y - mu
    var = jnp.mean(yc * yc, axis=-1, keepdims=True)
    z = yc * lax.rsqrt(var + EPS) * lg_ref[...] + lb_ref[...]
    o_ref[0] = (z * jax.nn.sigmoid(z)).astype(o_ref.dtype)


def _conv(proj, a_blk, b_blk, w, cb, lg, lb, ts):
    b, s, _ = proj.shape
    c = w.shape[1]
    hb = ts // CONV_HALO
    n_halo = s // CONV_HALO

    def main(col):
        return pl.BlockSpec((1, ts, c), lambda bi, i: (bi, i, col))

    def prev(col):
        return pl.BlockSpec((1, CONV_HALO, c), lambda bi, i: (bi, jnp.maximum(i * hb - 1, 0), col))

    def nxt(col):
        return pl.BlockSpec((1, CONV_HALO, c), lambda bi, i: (bi, jnp.minimum((i + 1) * hb, n_halo - 1), col))

    vec = pl.BlockSpec((1, c), lambda bi, i: (0, 0))
    rows, strip = 64, 256
    return pl.pallas_call(
        functools.partial(_conv_kernel, ts=ts, rows=rows, strip=strip),
        grid=(b, s // ts),
        in_specs=[prev(a_blk), main(a_blk), nxt(a_blk), prev(b_blk), main(b_blk), nxt(b_blk),
                  pl.BlockSpec((CONV_WIDTH, c), lambda bi, i: (0, 0)), vec, vec, vec],
        out_specs=pl.BlockSpec((1, ts, c), lambda bi, i: (bi, i, 0)),
        out_shape=jax.ShapeDtypeStruct((b, s, c), BF16),
        scratch_shapes=[pltpu.VMEM((ts + 2 * CONV_HALO, c), F32), pltpu.VMEM((ts, c), F32),
                        pltpu.VMEM((8, rows + 2 * CONV_HALO - 8, strip), F32)],
        compiler_params=_params("parallel", "parallel"),
        name="conv",
    )(proj, proj, proj, proj, proj, proj, w, cb, lg, lb)


def _merge_kernel(att_ref, cv_ref, ga_ref, gc_ref, x_ref, g1_ref, wa_ref, wc_ref, wo_ref, o_ref):
    y_att = jnp.dot(att_ref[0], wa_ref[...], preferred_element_type=F32)
    y_conv = jnp.dot(cv_ref[0], wc_ref[...], preferred_element_type=F32)
    merged = (jax.nn.sigmoid(ga_ref[0].astype(F32)) * y_att
              + jax.nn.sigmoid(gc_ref[0].astype(F32)) * y_conv)
    y = jnp.dot(merged.astype(BF16), wo_ref[...], preferred_element_type=F32)
    o_ref[0] = x_ref[0] + g1_ref[0] * y


def _merge(att, cv, proj, ga_blk, gc_blk, x, g1, wa, wc, wo, tm):
    b, s, d = x.shape
    tile = lambda col: pl.BlockSpec((1, tm, d), lambda bi, i: (bi, i, col))
    wspec = pl.BlockSpec((d, d), lambda bi, i: (0, 0))
    return pl.pallas_call(
        _merge_kernel,
        grid=(b, s // tm),
        in_specs=[tile(0), tile(0), tile(ga_blk), tile(gc_blk), tile(0),
                  pl.BlockSpec((1, 1, d), lambda bi, i: (bi, 0, 0)), wspec, wspec, wspec],
        out_specs=tile(0),
        out_shape=jax.ShapeDtypeStruct((b, s, d), F32),
        compiler_params=_params("parallel", "parallel"),
        name="merge",
    )(att, cv, proj, proj, x, g1, wa, wc, wo)


N_EXTRACT = PEER_TOPK + 1
N_CAND = sum(N_EXTRACT // (k + 1) for k in range(N_EXTRACT))
N_CAND_ROWS = -(-N_CAND // 8) * 8
SEL_GROUPS = 2


LANES = 128


def _col_max(s):
    parts = [s[r:r + 8] for r in range(0, s.shape[0], 8)]
    while len(parts) > 1:
        parts = [jnp.maximum(parts[i], parts[i + 1]) if i + 1 < len(parts) else parts[i]
                 for i in range(0, len(parts), 2)]
    return jnp.max(parts[0], axis=0, keepdims=True)


def _top_values(s, n):
    out = []
    for r in range(n):
        m = _col_max(s)
        out.append(m)
        s = jnp.where(s == m, -jnp.inf, s)
    return out


def _sort_network(n):
    def merge(lo, hi, r):
        step = r * 2
        if step < hi - lo:
            yield from merge(lo, hi, step)
            yield from merge(lo + r, hi, step)
            yield from [(i, i + r) for i in range(lo + r, hi - r, step)]
        else:
            yield (lo, lo + r)

    def sort(lo, hi):
        if hi - lo >= 1:
            mid = lo + (hi - lo) // 2
            yield from sort(lo, mid)
            yield from sort(mid + 1, hi)
            yield from merge(lo, hi, 1)

    return list(sort(0, n - 1))


def _top_values_tiled(s, n):
    v = [s[r:r + 8] for r in range(0, s.shape[0], 8)]
    for i, j in _sort_network(len(v)):
        v[i], v[j] = jnp.maximum(v[i], v[j]), jnp.minimum(v[i], v[j])
    depth = len(v)
    out = []
    for r in range(n):
        m = jnp.max(v[0], axis=0, keepdims=True)
        out.append(m)
        hit = v[0] == m
        for k in range(min(n - 1 - r, depth)):
            v[k] = jnp.where(hit, v[k + 1] if k + 1 < depth else -jnp.inf, v[k])
    return out


def _lookup_by_value(x, keys, vals, default):
    out = jnp.full(x.shape, default, F32)
    for key, val in zip(keys, vals):
        out = jnp.where(x == key, val, out)
    return out


HI16 = 0xFFFF0000


def _bf16_bits_hi(x):
    return pltpu.bitcast(x.astype(BF16).astype(F32), jnp.uint32) & jnp.uint32(HI16)


def _dup16(x):
    hi = _bf16_bits_hi(x)
    return hi | lax.shift_right_logical(hi, jnp.uint32(16))


def _pack_halves(x):
    hi = _bf16_bits_hi(x)
    r = x.shape[0] // 2
    return lax.shift_right_logical(hi[:r], jnp.uint32(16)) | hi[r:]


def _unpack_halves(w):
    lo = pltpu.bitcast(lax.shift_left(w, jnp.uint32(16)), F32)
    hi = pltpu.bitcast(w & jnp.uint32(HI16), F32)
    return lo, hi


def _peer_sel_kernel(xs_ref, g_ref, sh_ref, sc_ref, wqt_ref, keys_ref,
                     hx_ref, rank_ref, e1_ref, cnt_ref, c_ref, q_ref, s_ref, cand_ref, *, t):
    hb = _rms_modulate(xs_ref[0], g_ref[...], sh_ref[0], sc_ref[0]).astype(BF16)
    hx_ref[0] = hb
    q_ref[...] = lax.dot_general(wqt_ref[...], hb, NT_DIMS, preferred_element_type=F32).astype(BF16)
    cand_ref[...] = jnp.full(cand_ref.shape, -jnp.inf, F32)

    def head(hd, carry):
        for p in range(2):
            hp = hd * 2 + p
            q = q_ref[pl.ds(pl.multiple_of(hp * PEER_HALF, PEER_HALF), PEER_HALF), :]
            s_ref[p] = jnp.dot(keys_ref[hp], q, preferred_element_type=F32)

        def group(tg, slot):
            lanes = pl.ds(pl.multiple_of(tg * LANES, LANES), LANES)
            s0 = s_ref[0, :, lanes]
            s1 = s_ref[1, :, lanes]
            a0 = _top_values_tiled(s0, N_EXTRACT)
            a1 = _top_values_tiled(s1, N_EXTRACT)
            r = 0
            for k in range(N_EXTRACT):
                for l in range(N_EXTRACT // (k + 1)):
                    cand_ref[slot, r:r + 1, :] = a0[k] + a1[l]
                    r += 1
            best = _top_values(cand_ref[slot], N_EXTRACT)
            tau = 0.5 * (best[PEER_TOPK - 1] + best[PEER_TOPK])
            z = jnp.ones_like(tau)
            for r in range(1, PEER_TOPK):
                z = z + jnp.exp(best[r] - best[0])
            admitted = []
            for k in range(PEER_TOPK):
                n_k = jnp.zeros_like(tau)
                for l in range(N_EXTRACT // (k + 1)):
                    n_k = n_k + jnp.where(a0[k] + a1[l] >= tau, 1.0, 0.0)
                admitted.append(n_k)
            cnt = _lookup_by_value(s0, a0[:PEER_TOPK], admitted, 0.0)
            rank1 = _lookup_by_value(s1, a1, [float(l) for l in range(N_EXTRACT)], float(N_EXTRACT))
            rank_ref[0, hd, :, lanes] = _pack_halves(rank1)
            e1_ref[0, hd, :, lanes] = _pack_halves(jnp.exp(s1 - a1[0]))
            cnt_ref[0, hd, :, lanes] = _dup16(cnt)
            c_ref[0, hd, :, lanes] = _dup16(jnp.exp(s0 - a0[0]) / z)

        def groups(it, carry2):
            for slot in range(SEL_GROUPS):
                group(it * SEL_GROUPS + slot, slot)
            return carry2

        lax.fori_loop(0, t // (LANES * SEL_GROUPS), groups, 0)
        return carry

    lax.fori_loop(0, PEER_HEADS, head, 0)


def _peer_sel(xs, g, sh, sc, wqt, keys, t):
    b, s, d = xs.shape
    nq = wqt.shape[0]
    sel = pl.BlockSpec((1, PEER_HEADS, PEER_N_KEYS, t), lambda bi, i: (bi, 0, 0, i))
    sel_pk = pl.BlockSpec((1, PEER_HEADS, PEER_N_KEYS // 2, t), lambda bi, i: (bi, 0, 0, i))
    sel_u32 = jax.ShapeDtypeStruct((b, PEER_HEADS, PEER_N_KEYS, s), jnp.uint32)
    pk_u32 = jax.ShapeDtypeStruct((b, PEER_HEADS, PEER_N_KEYS // 2, s), jnp.uint32)
    return pl.pallas_call(
        functools.partial(_peer_sel_kernel, t=t),
        grid=(b, s // t),
        in_specs=[pl.BlockSpec((1, t, d), lambda bi, i: (bi, i, 0)),
                  pl.BlockSpec((1, d), lambda bi, i: (0, 0)),
                  pl.BlockSpec((1, 1, d), lambda bi, i: (bi, 0, 0)),
                  pl.BlockSpec((1, 1, d), lambda bi, i: (bi, 0, 0)),
                  pl.BlockSpec((nq, d), lambda bi, i: (0, 0)),
                  pl.BlockSpec((2 * PEER_HEADS, PEER_N_KEYS, PEER_HALF), lambda bi, i: (0, 0, 0))],
        out_specs=[pl.BlockSpec((1, t, d), lambda bi, i: (bi, i, 0)), sel_pk, sel_pk, sel, sel],
        out_shape=[jax.ShapeDtypeStruct((b, s, d), BF16), pk_u32, pk_u32, sel_u32, sel_u32],
        scratch_shapes=[pltpu.VMEM((nq, t), BF16), pltpu.VMEM((2, PEER_N_KEYS, t), F32),
                        pltpu.VMEM((SEL_GROUPS, N_CAND_ROWS, LANES), F32)],
        compiler_params=_params("parallel", "parallel"),
        name="peer_sel",
    )(xs, g, sh, sc, wqt, keys)


PK_ROWS = 32
W_BLKS = 4
MM_COLS = 256


def _bcast16(row, rows):
    return pltpu.bitcast(jnp.broadcast_to(row, (rows, LANES)), BF16)


def _peer_ffn_kernel(hx_ref, u_ref, vt_ref, vtl_ref, cnt_ref, c_ref, rank_ref, e1_ref, xs_ref, g2_ref, fg_ref,
                     o_ref, acc_ref, act_ref, w_ref, p_ref, *, ib, t):
    ec = pl.program_id(2)
    half = PEER_N_KEYS // 2

    @pl.when(ec == 0)
    def _():
        acc_ref[...] = jnp.zeros_like(acc_ref)
        p_ref[...] = jnp.zeros_like(p_ref)

    def weights(tg, jh, blks):
        lanes = slice(tg * LANES, (tg + 1) * LANES)
        jrows = slice(jh * PK_ROWS, (jh + 1) * PK_ROWS)
        w = {blk: jnp.zeros((2 * PK_ROWS, LANES), BF16) for blk in blks}
        for hd in range(PEER_HEADS):
            rank = pltpu.bitcast(rank_ref[0, hd, jrows, lanes], BF16)
            e1 = pltpu.bitcast(e1_ref[0, hd, jrows, lanes], BF16)
            for blk in blks:
                cnt = _bcast16(cnt_ref[0, hd, blk:blk + 1, lanes], PK_ROWS)
                c = _bcast16(c_ref[0, hd, blk:blk + 1, lanes], PK_ROWS)
                w[blk] = w[blk] + jnp.where(rank < cnt, e1, jnp.zeros_like(e1)) * c
        for blk in blks:
            w_ref[blk * half + jh * PK_ROWS:blk * half + (jh + 1) * PK_ROWS, lanes] = (
                pltpu.bitcast(w[blk], jnp.uint32))

    units = [(tg, jh, b0) for tg in range(t // LANES) for jh in range(half // PK_ROWS)
             for b0 in range(0, ib, W_BLKS)]
    n_cols = t // MM_COLS
    per = -(-len(units) // (2 * n_cols))
    for n in range(2 * n_cols):
        cols = slice((n // 2) * MM_COLS, (n // 2 + 1) * MM_COLS)
        if n % 2 == 0:
            act_ref[:, cols] = lax.dot_general(u_ref[...], hx_ref[0, cols, :], NT_DIMS,
                                               preferred_element_type=F32)
        else:
            acc_ref[:, cols] += jnp.dot(vt_ref[...], p_ref[:, cols], preferred_element_type=F32)
        for tg, jh, b0 in units[n * per:(n + 1) * per]:
            weights(tg, jh, range(b0, b0 + W_BLKS))

    for blk in range(ib):
        w_lo, w_hi = _unpack_halves(w_ref[blk * half:(blk + 1) * half, :])
        for part, wj in enumerate((w_lo, w_hi)):
            rows = slice(blk * PEER_N_KEYS + part * half, blk * PEER_N_KEYS + (part + 1) * half)
            a = act_ref[rows, :]
            gelu = 0.5 * a * (1.0 + lax.erf(a * (1.0 / math.sqrt(2.0))))
            p_ref[rows, :] = (wj * gelu).astype(BF16)

    @pl.when(ec == pl.num_programs(2) - 1)
    def _():
        acc = acc_ref[...] + jnp.dot(vtl_ref[...], p_ref[...], preferred_element_type=F32)
        x = xs_ref[0] + g2_ref[0] * acc.T
        ms = jnp.mean(x * x, axis=-1, keepdims=True)
        o_ref[0] = x * lax.rsqrt(ms + EPS) * fg_ref[...]


def _peer_ffn(hx, u, vt, cnt, c, rank, e1, xs, g2, fg, t, ib):
    b, s, d = xs.shape
    n_exp = u.shape[0]
    ecw = ib * PEER_N_KEYS
    n_chunks = n_exp // ecw
    row_sel = pl.BlockSpec((1, PEER_HEADS, ib, t), lambda bi, i, e: (bi, 0, e, i))
    all_sel = pl.BlockSpec((1, PEER_HEADS, PEER_N_KEYS // 2, t), lambda bi, i, e: (bi, 0, 0, i))
    return pl.pallas_call(
        functools.partial(_peer_ffn_kernel, ib=ib, t=t),
        grid=(b, s // t, n_chunks),
        in_specs=[pl.BlockSpec((1, t, d), lambda bi, i, e: (bi, i, 0)),
                  pl.BlockSpec((ecw, d), lambda bi, i, e: (e, 0)),
                  pl.BlockSpec((d, ecw), lambda bi, i, e: (0, jnp.maximum(e - 1, 0))),
                  pl.BlockSpec((d, ecw), lambda bi, i, e: (0, n_chunks - 1)),
                  row_sel, row_sel, all_sel, all_sel,
                  pl.BlockSpec((1, t, d), lambda bi, i, e: (bi, i, 0)),
                  pl.BlockSpec((1, 1, d), lambda bi, i, e: (bi, 0, 0)),
                  pl.BlockSpec((1, d), lambda bi, i, e: (0, 0))],
        out_specs=pl.BlockSpec((1, t, d), lambda bi, i, e: (bi, i, 0)),
        out_shape=jax.ShapeDtypeStruct((b, s, d), F32),
        scratch_shapes=[pltpu.VMEM((d, t), F32), pltpu.VMEM((ecw, t), F32),
                        pltpu.VMEM((ecw // 2, t), jnp.uint32), pltpu.VMEM((ecw, t), BF16)],
        compiler_params=_params("parallel", "parallel", "arbitrary"),
        name="peer_ffn",
    )(hx, u, vt, vt, cnt, c, rank, e1, xs, g2, fg)


def kernel(x, c, ctx, c_ctx, w_mod, b_mod, norm1_g, norm2_g, w_in, q_norm_g, k_norm_g, w_attn_o, conv_dw,
           conv_b, conv_ln_g, conv_ln_b, w_conv_o, w_out, peer_wq, peer_keys, peer_u, peer_v, final_norm_g):
    b, s, d = x.shape
    n_ctx = ctx.shape[1]
    assert w_mod.shape[0] == 1, "single-layer stack"
    qw = N_Q_HEADS * HEAD_DIM
    kvw = N_KV_HEADS * HEAD_DIM
    assert d == qw and conv_dw.shape[2] == d

    m_pad = -(-(b + 1) // 8) * 8
    cc = jnp.zeros((m_pad, d), F32).at[:b].set(c).at[b].set(c_ctx)
    mod = _mod(cc, w_mod[0], b_mod[0][None])
    sh1, sc1, g1, sh2, sc2, g2 = [mod[:b, None, j * d:(j + 1) * d] for j in range(N_MOD)]
    csh1, csc1 = [jnp.broadcast_to(mod[b:b + 1, None, j * d:(j + 1) * d], (b, 1, d)) for j in range(2)]

    wi = w_in[0]
    o_k, o_v, o_u, o_ga, o_gc = qw, qw + kvw, qw + 2 * kvw, qw + 2 * kvw + 2 * d, qw + 2 * kvw + 3 * d
    w_cat = jnp.concatenate([wi[:, :qw], wi[:, o_u:o_u + 2 * d], wi[:, o_ga:o_gc], wi[:, o_gc:],
                             wi[:, o_k:o_v], wi[:, o_v:o_u]], axis=1).astype(BF16)
    a_blk, b_blk, ga_blk, gc_blk = 1, 2, 3, 4
    kv_col = 5 * d

    n1 = norm1_g[0][None]
    tm = min(512, s)
    proj = _in_proj(x, n1, sh1, sc1, w_cat, tm)
    proj_c = _in_proj(ctx, n1, csh1, csc1, w_cat[:, kv_col:], min(256, n_ctx))

    cos, sin = _rope_tables(s)
    qg, kg = q_norm_g[0][None], k_norm_g[0][None]
    q = _head_prep(proj, 0, N_Q_HEADS, cos, sin, qg, HEAD_DIM ** -0.5, tm)
    k_lat = _head_prep(proj, kv_col // kvw, N_KV_HEADS, cos, sin, kg, 1.0, tm)
    k_ctx = _head_prep(proj_c, 0, N_KV_HEADS, jnp.ones((n_ctx, HEAD_DIM), F32),
                       jnp.zeros((n_ctx, HEAD_DIM), F32), kg, 1.0, min(256, n_ctx))
    k_all = jnp.concatenate([k_lat, k_ctx], axis=1)
    v_all = jnp.concatenate([proj[:, :, kv_col + kvw:], proj_c[:, :, kvw:]], axis=1)
    att = _attention(q, k_all, v_all, min(256, s))

    cv = _conv(proj, a_blk, b_blk, conv_dw[0], conv_b[0][None], conv_ln_g[0][None], conv_ln_b[0][None], tm)
    xs = _merge(att, cv, proj, ga_blk, gc_blk, x, g1, w_attn_o[0].astype(BF16), w_conv_o[0].astype(BF16),
                w_out[0].astype(BF16), tm)

    tp = min(512, s)
    wqt = peer_wq[0].T.astype(BF16)
    keys = peer_keys[0].reshape(2 * PEER_HEADS, PEER_N_KEYS, PEER_HALF).astype(BF16)
    hx, rank, e1, cnt, cw = _peer_sel(xs, norm2_g[0][None], sh2, sc2, wqt, keys, tp)
    u_b = peer_u[0].astype(BF16)
    vt_b = peer_v[0].T.astype(BF16)
    return _peer_ffn(hx, u_b, vt_b, cnt, cw, rank, e1, xs, g2, final_norm_g[None], tp, 8)
```

```python
import functools
import math

import jax
import jax.numpy as jnp
from jax import lax
from jax.experimental import pallas as pl
from jax.experimental.pallas import tpu as pltpu

F32 = jnp.float32
BF16 = jnp.bfloat16

EPS = 1e-6
HEAD_DIM = 128
N_Q_HEADS = 8
N_KV_HEADS = 2
GQA_GROUP = N_Q_HEADS // N_KV_HEADS
GRID_W = 64
ROPE_THETA = 10000.0
ROPE_AXIS_DIM = HEAD_DIM // 2
CONV_WIDTH = 31
CONV_PAD = CONV_WIDTH // 2
CONV_HALO = 16
N_MOD = 6
PEER_HEADS = 8
PEER_N_KEYS = 128
PEER_TOPK = 16
PEER_HALF = 128

VMEM_LIMIT_BYTES = 56 * 1024 * 1024

NT_DIMS = (((1,), (1,)), ((), ()))


def _params(*semantics, flags=None):
    return pltpu.CompilerParams(dimension_semantics=semantics, vmem_limit_bytes=VMEM_LIMIT_BYTES, flags=flags)


def _rms_modulate(x, g, shift, scale):
    ms = jnp.mean(x * x, axis=-1, keepdims=True)
    y = x * lax.rsqrt(ms + EPS) * g
    return y * (1.0 + scale) + shift


def _mod_kernel(c_ref, w_ref, b_ref, o_ref):
    c = c_ref[...]
    a = c * jax.nn.sigmoid(c)
    o_ref[...] = jnp.dot(a, w_ref[...], preferred_element_type=F32,
                         precision=lax.Precision.HIGHEST) + b_ref[...]


def _mod(cc, w, b):
    m, d = cc.shape
    n = w.shape[1]
    tn = n // 4
    return pl.pallas_call(
        _mod_kernel,
        grid=(n // tn,),
        in_specs=[pl.BlockSpec((m, d), lambda j: (0, 0)),
                  pl.BlockSpec((d, tn), lambda j: (0, j)),
                  pl.BlockSpec((1, tn), lambda j: (0, j))],
        out_specs=pl.BlockSpec((m, tn), lambda j: (0, j)),
        out_shape=jax.ShapeDtypeStruct((m, n), F32),
        compiler_params=_params("arbitrary"),
        name="mod",
    )(cc, w, b)


def _in_proj_kernel(x_ref, g_ref, sh_ref, sc_ref, w_ref, o_ref):
    h = _rms_modulate(x_ref[0], g_ref[...], sh_ref[0], sc_ref[0]).astype(BF16)
    o_ref[0] = jnp.dot(h, w_ref[...], preferred_element_type=F32).astype(o_ref.dtype)


def _in_proj(x, g, sh, sc, w, tm):
    b, s, d = x.shape
    n = w.shape[1]
    return pl.pallas_call(
        _in_proj_kernel,
        grid=(b, s // tm),
        in_specs=[pl.BlockSpec((1, tm, d), lambda bi, i: (bi, i, 0)),
                  pl.BlockSpec((1, d), lambda bi, i: (0, 0)),
                  pl.BlockSpec((1, 1, d), lambda bi, i: (bi, 0, 0)),
                  pl.BlockSpec((1, 1, d), lambda bi, i: (bi, 0, 0)),
                  pl.BlockSpec((d, n), lambda bi, i: (0, 0))],
        out_specs=pl.BlockSpec((1, tm, n), lambda bi, i: (bi, i, 0)),
        out_shape=jax.ShapeDtypeStruct((b, s, n), BF16),
        compiler_params=_params("parallel", "parallel"),
        name="in_proj",
    )(x, g, sh, sc, w)


def _head_prep_kernel(x_ref, cos_ref, sin_ref, g_ref, o_ref, *, n_heads, scale):
    cos = cos_ref[...]
    sin = sin_ref[...]
    g = g_ref[...] * scale
    lane = lax.broadcasted_iota(jnp.int32, cos.shape, 1)
    first = (lane % ROPE_AXIS_DIM) < (ROPE_AXIS_DIM // 2)
    for hd in range(n_heads):
        x = x_ref[0, :, hd * HEAD_DIM:(hd + 1) * HEAD_DIM].astype(F32)
        ms = jnp.mean(x * x, axis=-1, keepdims=True)
        y = x * lax.rsqrt(ms + EPS) * g
        partner = jnp.where(first, pltpu.roll(y, HEAD_DIM - ROPE_AXIS_DIM // 2, 1),
                            pltpu.roll(y, ROPE_AXIS_DIM // 2, 1))
        o_ref[0, :, hd * HEAD_DIM:(hd + 1) * HEAD_DIM] = (y * cos + partner * sin).astype(o_ref.dtype)


def _head_prep(x, col_block, n_heads, cos, sin, g, scale, tm):
    b, s, _ = x.shape
    w = n_heads * HEAD_DIM
    return pl.pallas_call(
        functools.partial(_head_prep_kernel, n_heads=n_heads, scale=scale),
        grid=(b, s // tm),
        in_specs=[pl.BlockSpec((1, tm, w), lambda bi, i: (bi, i, col_block)),
                  pl.BlockSpec((tm, HEAD_DIM), lambda bi, i: (i, 0)),
                  pl.BlockSpec((tm, HEAD_DIM), lambda bi, i: (i, 0)),
                  pl.BlockSpec((1, HEAD_DIM), lambda bi, i: (0, 0))],
        out_specs=pl.BlockSpec((1, tm, w), lambda bi, i: (bi, i, 0)),
        out_shape=jax.ShapeDtypeStruct((b, s, w), BF16),
        compiler_params=_params("parallel", "parallel"),
        name="head_prep",
    )(x, cos, sin, g)


def _rope_tables(s):
    t = jnp.arange(s, dtype=jnp.int32)
    row = (t // GRID_W).astype(F32)
    col = (t % GRID_W).astype(F32)
    inv = ROPE_THETA ** (-jnp.arange(0, ROPE_AXIS_DIM, 2, dtype=F32) / ROPE_AXIS_DIM)
    ang_r = row[:, None] * inv
    ang_c = col[:, None] * inv
    cos = jnp.concatenate([jnp.cos(ang_r), jnp.cos(ang_r), jnp.cos(ang_c), jnp.cos(ang_c)], axis=-1)
    sin = jnp.concatenate([-jnp.sin(ang_r), jnp.sin(ang_r), -jnp.sin(ang_c), jnp.sin(ang_c)], axis=-1)
    return cos, sin


def _attention_kernel(q_ref, k_ref, v_ref, o_ref):
    k = k_ref[0]
    v = v_ref[0]
    for g in range(GQA_GROUP):
        q = q_ref[0, :, g * HEAD_DIM:(g + 1) * HEAD_DIM]
        s = lax.dot_general(q, k, NT_DIMS, preferred_element_type=F32)
        m = jnp.max(s, axis=-1, keepdims=True)
        p = jnp.exp(s - m)
        l = jnp.sum(p, axis=-1, keepdims=True)
        o = jnp.dot(p.astype(BF16), v, preferred_element_type=F32)
        o_ref[0, :, g * HEAD_DIM:(g + 1) * HEAD_DIM] = (o / l).astype(o_ref.dtype)


def _attention(q, k, v, tq):
    b, s, _ = q.shape
    skv = k.shape[1]
    gw = GQA_GROUP * HEAD_DIM
    return pl.pallas_call(
        _attention_kernel,
        grid=(b, N_KV_HEADS, s // tq),
        in_specs=[pl.BlockSpec((1, tq, gw), lambda bi, kh, i: (bi, i, kh)),
                  pl.BlockSpec((1, skv, HEAD_DIM), lambda bi, kh, i: (bi, 0, kh)),
                  pl.BlockSpec((1, skv, HEAD_DIM), lambda bi, kh, i: (bi, 0, kh))],
        out_specs=pl.BlockSpec((1, tq, gw), lambda bi, kh, i: (bi, i, kh)),
        out_shape=jax.ShapeDtypeStruct((b, s, N_Q_HEADS * HEAD_DIM), BF16),
        compiler_params=_params("parallel", "parallel", "parallel"),
        name="attention",
    )(q, k, v)


def _conv_kernel(ap_ref, a_ref, an_ref, bp_ref, b_ref, bn_ref, w_ref, cb_ref, lg_ref, lb_ref,
                 o_ref, gext_ref, acc_ref, sh_ref, *, ts, rows, strip):
    i = pl.program_id(1)
    last = pl.num_programs(1) - 1

    def glu(a, b):
        return a.astype(F32) * jax.nn.sigmoid(b.astype(F32))

    gext_ref[0:CONV_HALO, :] = jnp.where(i > 0, glu(ap_ref[0], bp_ref[0]), 0.0)
    gext_ref[CONV_HALO:CONV_HALO + ts, :] = glu(a_ref[0], b_ref[0])
    gext_ref[CONV_HALO + ts:, :] = jnp.where(i < last, glu(an_ref[0], bn_ref[0]), 0.0)

    c = a_ref.shape[-1]
    off = CONV_HALO - CONV_PAD

    def chunk(r, carry):
        r0 = pl.multiple_of(r * rows, rows)
        for cs in range(c // strip):
            lanes = slice(cs * strip, (cs + 1) * strip)
            win = gext_ref[pl.ds(r0, rows + 2 * CONV_HALO), lanes]
            for r in range(8):
                sh_ref[r] = win[r:r + rows + 2 * CONV_HALO - 8, :]
            acc = jnp.zeros((rows, strip), F32)
            for k in range(CONV_WIDTH):
                q, r = divmod(off + k, 8)
                acc = acc + sh_ref[r, 8 * q:8 * q + rows, :] * w_ref[k:k + 1, lanes]
            acc_ref[pl.ds(r0, rows), lanes] = acc
        return carry

    lax.fori_loop(0, ts // rows, chunk, 0)

    y = acc_ref[...] + cb_ref[...]
    mu = jnp.mean(y, axis=-1, keepdims=True)
    yc = y - mu
    var = jnp.mean(yc * yc, axis=-1, keepdims=True)
    z = yc * lax.rsqrt(var + EPS) * lg_ref[...] + lb_ref[...]
    o_ref[0] = (z * jax.nn.sigmoid(z)).astype(o_ref.dtype)


def _conv(proj, a_blk, b_blk, w, cb, lg, lb, ts):
    b, s, _ = proj.shape
    c = w.shape[1]
    hb = ts // CONV_HALO
    n_halo = s // CONV_HALO

    def main(col):
        return pl.BlockSpec((1, ts, c), lambda bi, i: (bi, i, col))

    def prev(col):
        return pl.BlockSpec((1, CONV_HALO, c), lambda bi, i: (bi, jnp.maximum(i * hb - 1, 0), col))

    def nxt(col):
        return pl.BlockSpec((1, CONV_HALO, c), lambda bi, i: (bi, jnp.minimum((i + 1) * hb, n_halo - 1), col))

    vec = pl.BlockSpec((1, c), lambda bi, i: (0, 0))
    rows, strip = 64, 256
    return pl.pallas_call(
        functools.partial(_conv_kernel, ts=ts, rows=rows, strip=strip),
        grid=(b, s // ts),
        in_specs=[prev(a_blk), main(a_blk), nxt(a_blk), prev(b_blk), main(b_blk), nxt(b_blk),
                  pl.BlockSpec((CONV_WIDTH, c), lambda bi, i: (0, 0)), vec, vec, vec],
        out_specs=pl.BlockSpec((1, ts, c), lambda bi, i: (bi, i, 0)),
        out_shape=jax.ShapeDtypeStruct((b, s, c), BF16),
        scratch_shapes=[pltpu.VMEM((ts + 2 * CONV_HALO, c), F32), pltpu.VMEM((ts, c), F32),
                        pltpu.VMEM((8, rows + 2 * CONV_HALO - 8, strip), F32)],
        compiler_params=_params("parallel", "parallel"),
        name="conv",
    )(proj, proj, proj, proj, proj, proj, w, cb, lg, lb)


def _merge_kernel(att_ref, cv_ref, ga_ref, gc_ref, x_ref, g1_ref, wa_ref, wc_ref, wo_ref, o_ref):
    y_att = jnp.dot(att_ref[0], wa_ref[...], preferred_element_type=F32)
    y_conv = jnp.dot(cv_ref[0], wc_ref[...], preferred_element_type=F32)
    merged = (jax.nn.sigmoid(ga_ref[0].astype(F32)) * y_att
              + jax.nn.sigmoid(gc_ref[0].astype(F32)) * y_conv)
    y = jnp.dot(merged.astype(BF16), wo_ref[...], preferred_element_type=F32)
    o_ref[0] = x_ref[0] + g1_ref[0] * y


def _merge(att, cv, proj, ga_blk, gc_blk, x, g1, wa, wc, wo, tm):
    b, s, d = x.shape
    tile = lambda col: pl.BlockSpec((1, tm, d), lambda bi, i: (bi, i, col))
    wspec = pl.BlockSpec((d, d), lambda bi, i: (0, 0))
    return pl.pallas_call(
        _merge_kernel,
        grid=(b, s // tm),
        in_specs=[tile(0), tile(0), tile(ga_blk), tile(gc_blk), tile(0),
                  pl.BlockSpec((1, 1, d), lambda bi, i: (bi, 0, 0)), wspec, wspec, wspec],
        out_specs=tile(0),
        out_shape=jax.ShapeDtypeStruct((b, s, d), F32),
        compiler_params=_params("parallel", "parallel"),
        name="merge",
    )(att, cv, proj, proj, x, g1, wa, wc, wo)


N_EXTRACT = PEER_TOPK + 1
N_CAND = sum(N_EXTRACT // (k + 1) for k in range(N_EXTRACT))
N_CAND_ROWS = -(-N_CAND // 8) * 8
SEL_GROUPS = 2


LANES = 128


def _col_max(s):
    parts = [s[r:r + 8] for r in range(0, s.shape[0], 8)]
    while len(parts) > 1:
        parts = [jnp.maximum(parts[i], parts[i + 1]) if i + 1 < len(parts) else parts[i]
                 for i in range(0, len(parts), 2)]
    return jnp.max(parts[0], axis=0, keepdims=True)


def _top_values(s, n):
    out = []
    for r in range(n):
        m = _col_max(s)
        out.append(m)
        s = jnp.where(s == m, -jnp.inf, s)
    return out


def _sort_network(n):
    def merge(lo, hi, r):
        step = r * 2
        if step < hi - lo:
            yield from merge(lo, hi, step)
            yield from merge(lo + r, hi, step)
            yield from [(i, i + r) for i in range(lo + r, hi - r, step)]
        else:
            yield (lo, lo + r)

    def sort(lo, hi):
        if hi - lo >= 1:
            mid = lo + (hi - lo) // 2
            yield from sort(lo, mid)
            yield from sort(mid + 1, hi)
            yield from merge(lo, hi, 1)

    return list(sort(0, n - 1))


def _top_values_tiled(s, n):
    v = [s[r:r + 8] for r in range(0, s.shape[0], 8)]
    for i, j in _sort_network(len(v)):
        v[i], v[j] = jnp.maximum(v[i], v[j]), jnp.minimum(v[i], v[j])
    depth = len(v)
    out = []
    for r in range(n):
        m = jnp.max(v[0], axis=0, keepdims=True)
        out.append(m)
        hit = v[0] == m
        for k in range(min(n - 1 - r, depth)):
            v[k] = jnp.where(hit, v[k + 1] if k + 1 < depth else -jnp.inf, v[k])
    return out


def _lookup_by_value(x, keys, vals, default):
    out = jnp.full(x.shape, default, F32)
    for key, val in zip(keys, vals):
        out = jnp.where(x == key, val, out)
    return out


HI16 = 0xFFFF0000


def _bf16_bits_hi(x):
    return pltpu.bitcast(x.astype(BF16).astype(F32), jnp.uint32) & jnp.uint32(HI16)


def _dup16(x):
    hi = _bf16_bits_hi(x)
    return hi | lax.shift_right_logical(hi, jnp.uint32(16))


def _pack_halves(x):
    hi = _bf16_bits_hi(x)
    r = x.shape[0] // 2
    return lax.shift_right_logical(hi[:r], jnp.uint32(16)) | hi[r:]


def _unpack_halves(w):
    lo = pltpu.bitcast(lax.shift_left(w, jnp.uint32(16)), F32)
    hi = pltpu.bitcast(w & jnp.uint32(HI16), F32)
    return lo, hi


def _peer_sel_kernel(xs_ref, g_ref, sh_ref, sc_ref, wqt_ref, keys_ref,
                     hx_ref, rank_ref, e1_ref, cnt_ref, c_ref, q_ref, s_ref, cand_ref, *, t):
    ht = _rms_modulate(xs_ref[0], g_ref[...], sh_ref[0], sc_ref[0]).T.astype(BF16)
    hx_ref[0] = ht
    q_ref[...] = jnp.dot(wqt_ref[...], ht, preferred_element_type=F32).astype(BF16)
    cand_ref[...] = jnp.full(cand_ref.shape, -jnp.inf, F32)

    def head(hd, carry):
        for p in range(2):
            hp = hd * 2 + p
            q = q_ref[pl.ds(pl.multiple_of(hp * PEER_HALF, PEER_HALF), PEER_HALF), :]
            s_ref[p] = jnp.dot(keys_ref[hp], q, preferred_element_type=F32)

        def group(tg, slot):
            lanes = pl.ds(pl.multiple_of(tg * LANES, LANES), LANES)
            s0 = s_ref[0, :, lanes]
            s1 = s_ref[1, :, lanes]
            a0 = _top_values_tiled(s0, N_EXTRACT)
            a1 = _top_values_tiled(s1, N_EXTRACT)
            r = 0
            for k in range(N_EXTRACT):
                for l in range(N_EXTRACT // (k + 1)):
                    cand_ref[slot, r:r + 1, :] = a0[k] + a1[l]
                    r += 1
            best = _top_values(cand_ref[slot], N_EXTRACT)
            tau = 0.5 * (best[PEER_TOPK - 1] + best[PEER_TOPK])
            z = jnp.ones_like(tau)
            for r in range(1, PEER_TOPK):
                z = z + jnp.exp(best[r] - best[0])
            admitted = []
            for k in range(PEER_TOPK):
                n_k = jnp.zeros_like(tau)
                for l in range(N_EXTRACT // (k + 1)):
                    n_k = n_k + jnp.where(a0[k] + a1[l] >= tau, 1.0, 0.0)
                admitted.append(n_k)
            cnt = _lookup_by_value(s0, a0[:PEER_TOPK], admitted, 0.0)
            rank1 = _lookup_by_value(s1, a1, [float(l) for l in range(N_EXTRACT)], float(N_EXTRACT))
            rank_ref[0, hd, :, lanes] = _pack_halves(rank1)
            e1_ref[0, hd, :, lanes] = _pack_halves(jnp.exp(s1 - a1[0]))
            cnt_ref[0, hd, :, lanes] = _dup16(cnt)
            c_ref[0, hd, :, lanes] = _dup16(jnp.exp(s0 - a0[0]) * (0.5 / z))

        def groups(it, carry2):
            for slot in range(SEL_GROUPS):
                group(it * SEL_GROUPS + slot, slot)
            return carry2

        lax.fori_loop(0, t // (LANES * SEL_GROUPS), groups, 0)
        return carry

    lax.fori_loop(0, PEER_HEADS, head, 0)


def _peer_sel(xs, g, sh, sc, wqt, keys, t):
    b, s, d = xs.shape
    nq = wqt.shape[0]
    sel = pl.BlockSpec((1, PEER_HEADS, PEER_N_KEYS, t), lambda bi, i: (bi, 0, 0, i))
    sel_pk = pl.BlockSpec((1, PEER_HEADS, PEER_N_KEYS // 2, t), lambda bi, i: (bi, 0, 0, i))
    sel_u32 = jax.ShapeDtypeStruct((b, PEER_HEADS, PEER_N_KEYS, s), jnp.uint32)
    pk_u32 = jax.ShapeDtypeStruct((b, PEER_HEADS, PEER_N_KEYS // 2, s), jnp.uint32)
    return pl.pallas_call(
        functools.partial(_peer_sel_kernel, t=t),
        grid=(b, s // t),
        in_specs=[pl.BlockSpec((1, t, d), lambda bi, i: (bi, i, 0)),
                  pl.BlockSpec((1, d), lambda bi, i: (0, 0)),
                  pl.BlockSpec((1, 1, d), lambda bi, i: (bi, 0, 0)),
                  pl.BlockSpec((1, 1, d), lambda bi, i: (bi, 0, 0)),
                  pl.BlockSpec((nq, d), lambda bi, i: (0, 0)),
                  pl.BlockSpec((2 * PEER_HEADS, PEER_N_KEYS, PEER_HALF), lambda bi, i: (0, 0, 0))],
        out_specs=[pl.BlockSpec((1, d, t), lambda bi, i: (bi, 0, i)), sel_pk, sel_pk, sel, sel],
        out_shape=[jax.ShapeDtypeStruct((b, d, s), BF16), pk_u32, pk_u32, sel_u32, sel_u32],
        scratch_shapes=[pltpu.VMEM((nq, t), BF16), pltpu.VMEM((2, PEER_N_KEYS, t), F32),
                        pltpu.VMEM((SEL_GROUPS, N_CAND_ROWS, LANES), F32)],
        compiler_params=_params("parallel", "parallel"),
        name="peer_sel",
    )(xs, g, sh, sc, wqt, keys)


PK_ROWS = 32
W_BLKS = 4
MM_COLS = 256


def _bcast16(row, rows):
    return pltpu.bitcast(jnp.broadcast_to(row, (rows, LANES)), BF16)


def _peer_ffn_kernel(hx_ref, u_ref, vt_ref, vtl_ref, cnt_ref, c_ref, rank_ref, e1_ref, xs_ref, g2_ref, fg_ref,
                     o_ref, acc_ref, act_ref, w_ref, p_ref, *, ib, t):
    ec = pl.program_id(2)
    half = PEER_N_KEYS // 2

    @pl.when(ec == 0)
    def _():
        acc_ref[...] = jnp.zeros_like(acc_ref)
        p_ref[...] = jnp.zeros_like(p_ref)

    def weights(tg, jh, blks):
        lanes = slice(tg * LANES, (tg + 1) * LANES)
        jrows = slice(jh * PK_ROWS, (jh + 1) * PK_ROWS)
        subs = range(0, PK_ROWS, 8)
        w = {(blk, r): jnp.zeros((16, LANES), BF16) for blk in blks for r in subs}
        for hd in range(PEER_HEADS):
            rank_u = rank_ref[0, hd, jrows, lanes]
            e1_u = e1_ref[0, hd, jrows, lanes]
            rank = {r: pltpu.bitcast(rank_u[r:r + 8], BF16) for r in subs}
            e1 = {r: pltpu.bitcast(e1_u[r:r + 8], BF16) for r in subs}
            for blk in blks:
                cnt = _bcast16(cnt_ref[0, hd, blk:blk + 1, lanes], 8)
                c = _bcast16(c_ref[0, hd, blk:blk + 1, lanes], 8)
                for r in subs:
                    w[blk, r] = w[blk, r] + jnp.where(rank[r] < cnt, e1[r], jnp.zeros_like(cnt)) * c
        for blk in blks:
            for r in subs:
                w_ref[blk * half + jh * PK_ROWS + r:blk * half + jh * PK_ROWS + r + 8, lanes] = (
                    pltpu.bitcast(w[blk, r], jnp.uint32))

    units = [(tg, jh, b0) for tg in range(t // LANES) for jh in range(half // PK_ROWS)
             for b0 in range(0, ib, W_BLKS)]
    n_cols = t // MM_COLS
    per = -(-len(units) // (2 * n_cols))
    for n in range(2 * n_cols):
        cols = slice((n // 2) * MM_COLS, (n // 2 + 1) * MM_COLS)
        for tg, jh, b0 in units[n * per:(n + 1) * per]:
            weights(tg, jh, range(b0, b0 + W_BLKS))
        if n % 2 == 0:
            act_ref[:, cols] = jnp.dot(u_ref[...], hx_ref[0, :, cols], preferred_element_type=F32)
        else:
            acc_ref[:, cols] += jnp.dot(vt_ref[...], p_ref[:, cols], preferred_element_type=F32)

    for blk in range(ib):
        w_lo, w_hi = _unpack_halves(w_ref[blk * half:(blk + 1) * half, :])
        for part, wj in enumerate((w_lo, w_hi)):
            rows = slice(blk * PEER_N_KEYS + part * half, blk * PEER_N_KEYS + (part + 1) * half)
            a = act_ref[rows, :]
            p_ref[rows, :] = (wj * a * (1.0 + lax.erf(a * (1.0 / math.sqrt(2.0))))).astype(BF16)

    @pl.when(ec == pl.num_programs(2) - 1)
    def _():
        acc = acc_ref[...] + jnp.dot(vtl_ref[...], p_ref[...], preferred_element_type=F32)
        x = xs_ref[0] + g2_ref[0] * acc.T
        ms = jnp.mean(x * x, axis=-1, keepdims=True)
        o_ref[0] = x * lax.rsqrt(ms + EPS) * fg_ref[...]


def _peer_ffn(hx, u, vt, cnt, c, rank, e1, xs, g2, fg, t, ib):
    b, s, d = xs.shape
    n_exp = u.shape[0]
    ecw = ib * PEER_N_KEYS
    n_chunks = n_exp // ecw
    row_sel = pl.BlockSpec((1, PEER_HEADS, ib, t), lambda bi, i, e: (bi, 0, e, i))
    all_sel = pl.BlockSpec((1, PEER_HEADS, PEER_N_KEYS // 2, t), lambda bi, i, e: (bi, 0, 0, i))
    return pl.pallas_call(
        functools.partial(_peer_ffn_kernel, ib=ib, t=t),
        grid=(b, s // t, n_chunks),
        in_specs=[pl.BlockSpec((1, d, t), lambda bi, i, e: (bi, 0, i)),
                  pl.BlockSpec((ecw, d), lambda bi, i, e: (e, 0)),
                  pl.BlockSpec((d, ecw), lambda bi, i, e: (0, jnp.maximum(e - 1, 0))),
                  pl.BlockSpec((d, ecw), lambda bi, i, e: (0, n_chunks - 1)),
                  row_sel, row_sel, all_sel, all_sel,
                  pl.BlockSpec((1, t, d), lambda bi, i, e: (bi, i, 0)),
                  pl.BlockSpec((1, 1, d), lambda bi, i, e: (bi, 0, 0)),
                  pl.BlockSpec((1, d), lambda bi, i, e: (0, 0))],
        out_specs=pl.BlockSpec((1, t, d), lambda bi, i, e: (bi, i, 0)),
        out_shape=jax.ShapeDtypeStruct((b, s, d), F32),
        scratch_shapes=[pltpu.VMEM((d, t), F32), pltpu.VMEM((ecw, t), F32),
                        pltpu.VMEM((ecw // 2, t), jnp.uint32), pltpu.VMEM((ecw, t), BF16)],
        compiler_params=_params("parallel", "parallel", "arbitrary"),
        name="peer_ffn",
    )(hx, u, vt, vt, cnt, c, rank, e1, xs, g2, fg)


def kernel(x, c, ctx, c_ctx, w_mod, b_mod, norm1_g, norm2_g, w_in, q_norm_g, k_norm_g, w_attn_o, conv_dw,
           conv_b, conv_ln_g, conv_ln_b, w_conv_o, w_out, peer_wq, peer_keys, peer_u, peer_v, final_norm_g):
    b, s, d = x.shape
    n_ctx = ctx.shape[1]
    assert w_mod.shape[0] == 1, "single-layer stack"
    qw = N_Q_HEADS * HEAD_DIM
    kvw = N_KV_HEADS * HEAD_DIM
    assert d == qw and conv_dw.shape[2] == d

    m_pad = -(-(b + 1) // 8) * 8
    cc = jnp.zeros((m_pad, d), F32).at[:b].set(c).at[b].set(c_ctx)
    mod = _mod(cc, w_mod[0], b_mod[0][None])
    sh1, sc1, g1, sh2, sc2, g2 = [mod[:b, None, j * d:(j + 1) * d] for j in range(N_MOD)]
    csh1, csc1 = [jnp.broadcast_to(mod[b:b + 1, None, j * d:(j + 1) * d], (b, 1, d)) for j in range(2)]

    wi = w_in[0]
    o_k, o_v, o_u, o_ga, o_gc = qw, qw + kvw, qw + 2 * kvw, qw + 2 * kvw + 2 * d, qw + 2 * kvw + 3 * d
    w_cat = jnp.concatenate([wi[:, :qw], wi[:, o_u:o_u + 2 * d], wi[:, o_ga:o_gc], wi[:, o_gc:],
                             wi[:, o_k:o_v], wi[:, o_v:o_u]], axis=1).astype(BF16)
    a_blk, b_blk, ga_blk, gc_blk = 1, 2, 3, 4
    kv_col = 5 * d

    n1 = norm1_g[0][None]
    tm = min(512, s)
    proj = _in_proj(x, n1, sh1, sc1, w_cat, tm)
    proj_c = _in_proj(ctx, n1, csh1, csc1, w_cat[:, kv_col:], min(256, n_ctx))

    cos, sin = _rope_tables(s)
    qg, kg = q_norm_g[0][None], k_norm_g[0][None]
    q = _head_prep(proj, 0, N_Q_HEADS, cos, sin, qg, HEAD_DIM ** -0.5, tm)
    k_lat = _head_prep(proj, kv_col // kvw, N_KV_HEADS, cos, sin, kg, 1.0, tm)
    k_ctx = _head_prep(proj_c, 0, N_KV_HEADS, jnp.ones((n_ctx, HEAD_DIM), F32),
                       jnp.zeros((n_ctx, HEAD_DIM), F32), kg, 1.0, min(256, n_ctx))
    k_all = jnp.concatenate([k_lat, k_ctx], axis=1)
    v_all = jnp.concatenate([proj[:, :, kv_col + kvw:], proj_c[:, :, kvw:]], axis=1)
    att = _attention(q, k_all, v_all, min(256, s))

    cv = _conv(proj, a_blk, b_blk, conv_dw[0], conv_b[0][None], conv_ln_g[0][None], conv_ln_b[0][None], tm)
    xs = _merge(att, cv, proj, ga_blk, gc_blk, x, g1, w_attn_o[0].astype(BF16), w_conv_o[0].astype(BF16),
                w_out[0].astype(BF16), tm)

    tp = min(512, s)
    wqt = peer_wq[0].T.astype(BF16)
    keys = peer_keys[0].reshape(2 * PEER_HEADS, PEER_N_KEYS, PEER_HALF).astype(BF16)
    hx, rank, e1, cnt, cw = _peer_sel(xs, norm2_g[0][None], sh2, sc2, wqt, keys, tp)
    u_b = peer_u[0].astype(BF16)
    vt_b = peer_v[0].T.astype(BF16)
    return _peer_ffn(hx, u_b, vt_b, cnt, cw, rank, e1, xs, g2, final_norm_g[None], tp, 16)
```

```python
import functools
import math

import jax
import jax.numpy as jnp
from jax import lax
from jax.experimental import pallas as pl
from jax.experimental.pallas import tpu as pltpu

F32 = jnp.float32
BF16 = jnp.bfloat16

EPS = 1e-6
HEAD_DIM = 128
N_Q_HEADS = 8
N_KV_HEADS = 2
GQA_GROUP = N_Q_HEADS // N_KV_HEADS
GRID_W = 64
ROPE_THETA = 10000.0
ROPE_AXIS_DIM = HEAD_DIM // 2
CONV_WIDTH = 31
CONV_PAD = CONV_WIDTH // 2
CONV_HALO = 16
N_MOD = 6
PEER_HEADS = 8
PEER_N_KEYS = 128
PEER_TOPK = 16
PEER_HALF = 128

VMEM_LIMIT_BYTES = 56 * 1024 * 1024

NT_DIMS = (((1,), (1,)), ((), ()))


def _params(*semantics, flags=None):
    return pltpu.CompilerParams(dimension_semantics=semantics, vmem_limit_bytes=VMEM_LIMIT_BYTES, flags=flags)


def _rms_modulate(x, g, shift, scale):
    ms = jnp.mean(x * x, axis=-1, keepdims=True)
    y = x * lax.rsqrt(ms + EPS) * g
    return y * (1.0 + scale) + shift


def _mod_kernel(c_ref, w_ref, b_ref, o_ref):
    c = c_ref[...]
    a = c * jax.nn.sigmoid(c)
    o_ref[...] = jnp.dot(a, w_ref[...], preferred_element_type=F32,
                         precision=lax.Precision.HIGHEST) + b_ref[...]


def _mod(cc, w, b):
    m, d = cc.shape
    n = w.shape[1]
    tn = n // 4
    return pl.pallas_call(
        _mod_kernel,
        grid=(n // tn,),
        in_specs=[pl.BlockSpec((m, d), lambda j: (0, 0)),
                  pl.BlockSpec((d, tn), lambda j: (0, j)),
                  pl.BlockSpec((1, tn), lambda j: (0, j))],
        out_specs=pl.BlockSpec((m, tn), lambda j: (0, j)),
        out_shape=jax.ShapeDtypeStruct((m, n), F32),
        compiler_params=_params("arbitrary"),
        name="mod",
    )(cc, w, b)


def _in_proj_kernel(x_ref, g_ref, sh_ref, sc_ref, w_ref, o_ref):
    h = _rms_modulate(x_ref[0], g_ref[...], sh_ref[0], sc_ref[0]).astype(BF16)
    o_ref[0] = jnp.dot(h, w_ref[...], preferred_element_type=F32).astype(o_ref.dtype)


def _in_proj(x, g, sh, sc, w, tm):
    b, s, d = x.shape
    n = w.shape[1]
    return pl.pallas_call(
        _in_proj_kernel,
        grid=(b, s // tm),
        in_specs=[pl.BlockSpec((1, tm, d), lambda bi, i: (bi, i, 0)),
                  pl.BlockSpec((1, d), lambda bi, i: (0, 0)),
                  pl.BlockSpec((1, 1, d), lambda bi, i: (bi, 0, 0)),
                  pl.BlockSpec((1, 1, d), lambda bi, i: (bi, 0, 0)),
                  pl.BlockSpec((d, n), lambda bi, i: (0, 0))],
        out_specs=pl.BlockSpec((1, tm, n), lambda bi, i: (bi, i, 0)),
        out_shape=jax.ShapeDtypeStruct((b, s, n), BF16),
        compiler_params=_params("parallel", "parallel"),
        name="in_proj",
    )(x, g, sh, sc, w)


def _norm_rope(x, g, cos, sin):
    ms = jnp.mean(x * x, axis=-1, keepdims=True)
    y = x * lax.rsqrt(ms + EPS) * g
    lane = lax.broadcasted_iota(jnp.int32, y.shape, 1)
    first = (lane % ROPE_AXIS_DIM) < (ROPE_AXIS_DIM // 2)
    partner = jnp.where(first, pltpu.roll(y, HEAD_DIM - ROPE_AXIS_DIM // 2, 1),
                        pltpu.roll(y, ROPE_AXIS_DIM // 2, 1))
    return y * cos + partner * sin


def _head_prep_kernel(x_ref, cos_ref, sin_ref, g_ref, o_ref, *, n_heads, scale):
    cos = cos_ref[...]
    sin = sin_ref[...]
    g = g_ref[...] * scale
    for hd in range(n_heads):
        x = x_ref[0, :, hd * HEAD_DIM:(hd + 1) * HEAD_DIM].astype(F32)
        o_ref[0, :, hd * HEAD_DIM:(hd + 1) * HEAD_DIM] = _norm_rope(x, g, cos, sin).astype(o_ref.dtype)


def _head_prep(x, col_block, n_heads, cos, sin, g, scale, tm):
    b, s, _ = x.shape
    w = n_heads * HEAD_DIM
    return pl.pallas_call(
        functools.partial(_head_prep_kernel, n_heads=n_heads, scale=scale),
        grid=(b, s // tm),
        in_specs=[pl.BlockSpec((1, tm, w), lambda bi, i: (bi, i, col_block)),
                  pl.BlockSpec((tm, HEAD_DIM), lambda bi, i: (i, 0)),
                  pl.BlockSpec((tm, HEAD_DIM), lambda bi, i: (i, 0)),
                  pl.BlockSpec((1, HEAD_DIM), lambda bi, i: (0, 0))],
        out_specs=pl.BlockSpec((1, tm, w), lambda bi, i: (bi, i, 0)),
        out_shape=jax.ShapeDtypeStruct((b, s, w), BF16),
        compiler_params=_params("parallel", "parallel"),
        name="head_prep",
    )(x, cos, sin, g)


def _rope_tables(s):
    t = jnp.arange(s, dtype=jnp.int32)
    row = (t // GRID_W).astype(F32)
    col = (t % GRID_W).astype(F32)
    inv = ROPE_THETA ** (-jnp.arange(0, ROPE_AXIS_DIM, 2, dtype=F32) / ROPE_AXIS_DIM)
    ang_r = row[:, None] * inv
    ang_c = col[:, None] * inv
    cos = jnp.concatenate([jnp.cos(ang_r), jnp.cos(ang_r), jnp.cos(ang_c), jnp.cos(ang_c)], axis=-1)
    sin = jnp.concatenate([-jnp.sin(ang_r), jnp.sin(ang_r), -jnp.sin(ang_c), jnp.sin(ang_c)], axis=-1)
    return cos, sin


def _attention_kernel(q_ref, cos_ref, sin_ref, qg_ref, k_ref, v_ref, o_ref):
    k = k_ref[0]
    v1 = v_ref[0]
    cos = cos_ref[...]
    sin = sin_ref[...]
    qg = qg_ref[...] * (HEAD_DIM ** -0.5 * math.log2(math.e))
    for g in range(GQA_GROUP):
        x = q_ref[0, :, g * HEAD_DIM:(g + 1) * HEAD_DIM].astype(F32)
        q = _norm_rope(x, qg, cos, sin).astype(BF16)
        s = lax.dot_general(q, k, NT_DIMS, preferred_element_type=F32)
        p = jnp.exp2(s - jnp.max(s, axis=-1, keepdims=True))
        o = jnp.dot(p.astype(BF16), v1, preferred_element_type=F32)
        o_ref[0, :, g * HEAD_DIM:(g + 1) * HEAD_DIM] = (o[:, :HEAD_DIM] / o[:, HEAD_DIM:]).astype(o_ref.dtype)


def _attention(proj, cos, sin, qg, k, v, tq):
    b, s, _ = proj.shape
    skv = k.shape[1]
    gw = GQA_GROUP * HEAD_DIM
    return pl.pallas_call(
        _attention_kernel,
        grid=(b, N_KV_HEADS, s // tq),
        in_specs=[pl.BlockSpec((1, tq, gw), lambda bi, kh, i: (bi, i, kh)),
                  pl.BlockSpec((tq, HEAD_DIM), lambda bi, kh, i: (i, 0)),
                  pl.BlockSpec((tq, HEAD_DIM), lambda bi, kh, i: (i, 0)),
                  pl.BlockSpec((1, HEAD_DIM), lambda bi, kh, i: (0, 0)),
                  pl.BlockSpec((1, skv, HEAD_DIM), lambda bi, kh, i: (bi, 0, kh)),
                  pl.BlockSpec((1, skv, 2 * HEAD_DIM), lambda bi, kh, i: (bi, 0, kh))],
        out_specs=pl.BlockSpec((1, tq, gw), lambda bi, kh, i: (bi, i, kh)),
        out_shape=jax.ShapeDtypeStruct((b, s, N_Q_HEADS * HEAD_DIM), BF16),
        compiler_params=_params("parallel", "parallel", "parallel"),
        name="attention",
    )(proj, cos, sin, qg, k, v)


def _conv_kernel(ap_ref, a_ref, an_ref, bp_ref, b_ref, bn_ref, w_ref, cb_ref, lg_ref, lb_ref,
                 o_ref, gext_ref, acc_ref, sh_ref, *, ts, rows, strip):
    i = pl.program_id(1)
    last = pl.num_programs(1) - 1

    def glu(a, b):
        return a.astype(F32) * jax.nn.sigmoid(b.astype(F32))

    gext_ref[0:CONV_HALO, :] = jnp.where(i > 0, glu(ap_ref[0], bp_ref[0]), 0.0)
    gext_ref[CONV_HALO:CONV_HALO + ts, :] = glu(a_ref[0], b_ref[0])
    gext_ref[CONV_HALO + ts:, :] = jnp.where(i < last, glu(an_ref[0], bn_ref[0]), 0.0)

    c = a_ref.shape[-1]
    off = CONV_HALO - CONV_PAD

    def chunk(r, carry):
        r0 = pl.multiple_of(r * rows, rows)
        for cs in range(c // strip):
            lanes = slice(cs * strip, (cs + 1) * strip)
            win = gext_ref[pl.ds(r0, rows + 2 * CONV_HALO), lanes]
            for r in range(8):
                sh_ref[r] = win[r:r + rows + 2 * CONV_HALO - 8, :]
            acc = jnp.zeros((rows, strip), F32)
            for k in range(CONV_WIDTH):
                q, r = divmod(off + k, 8)
                acc = acc + sh_ref[r, 8 * q:8 * q + rows, :] * w_ref[k:k + 1, lanes]
            acc_ref[pl.ds(r0, rows), lanes] = acc
        return carry

    lax.fori_loop(0, ts // rows, chunk, 0)

    y = acc_ref[...] + cb_ref[...]
    mu = jnp.mean(y, axis=-1, keepdims=True)
    yc = y - mu
    var = jnp.mean(yc * yc, axis=-1, keepdims=True)
    z = yc * lax.rsqrt(var + EPS) * lg_ref[...] + lb_ref[...]
    o_ref[0] = (z * jax.nn.sigmoid(z)).astype(o_ref.dtype)


def _conv(proj, a_blk, b_blk, w, cb, lg, lb, ts):
    b, s, _ = proj.shape
    c = w.shape[1]
    hb = ts // CONV_HALO
    n_halo = s // CONV_HALO

    def main(col):
        return pl.BlockSpec((1, ts, c), lambda bi, i: (bi, i, col))

    def prev(col):
        return pl.BlockSpec((1, CONV_HALO, c), lambda bi, i: (bi, jnp.maximum(i * hb - 1, 0), col))

    def nxt(col):
        return pl.BlockSpec((1, CONV_HALO, c), lambda bi, i: (bi, jnp.minimum((i + 1) * hb, n_halo - 1), col))

    vec = pl.BlockSpec((1, c), lambda bi, i: (0, 0))
    rows, strip = 64, 256
    return pl.pallas_call(
        functools.partial(_conv_kernel, ts=ts, rows=rows, strip=strip),
        grid=(b, s // ts),
        in_specs=[prev(a_blk), main(a_blk), nxt(a_blk), prev(b_blk), main(b_blk), nxt(b_blk),
                  pl.BlockSpec((CONV_WIDTH, c), lambda bi, i: (0, 0)), vec, vec, vec],
        out_specs=pl.BlockSpec((1, ts, c), lambda bi, i: (bi, i, 0)),
        out_shape=jax.ShapeDtypeStruct((b, s, c), BF16),
        scratch_shapes=[pltpu.VMEM((ts + 2 * CONV_HALO, c), F32), pltpu.VMEM((ts, c), F32),
                        pltpu.VMEM((8, rows + 2 * CONV_HALO - 8, strip), F32)],
        compiler_params=_params("parallel", "parallel"),
        name="conv",
    )(proj, proj, proj, proj, proj, proj, w, cb, lg, lb)


def _merge_kernel(att_ref, cv_ref, ga_ref, gc_ref, x_ref, g1_ref, wa_ref, wc_ref, wo_ref, o_ref):
    y_att = jnp.dot(att_ref[0], wa_ref[...], preferred_element_type=F32)
    y_conv = jnp.dot(cv_ref[0], wc_ref[...], preferred_element_type=F32)
    merged = (jax.nn.sigmoid(ga_ref[0].astype(F32)) * y_att
              + jax.nn.sigmoid(gc_ref[0].astype(F32)) * y_conv)
    y = jnp.dot(merged.astype(BF16), wo_ref[...], preferred_element_type=F32)
    o_ref[0] = x_ref[0] + g1_ref[0] * y


def _merge(att, cv, proj, ga_blk, gc_blk, x, g1, wa, wc, wo, tm):
    b, s, d = x.shape
    tile = lambda col: pl.BlockSpec((1, tm, d), lambda bi, i: (bi, i, col))
    wspec = pl.BlockSpec((d, d), lambda bi, i: (0, 0))
    return pl.pallas_call(
        _merge_kernel,
        grid=(b, s // tm),
        in_specs=[tile(0), tile(0), tile(ga_blk), tile(gc_blk), tile(0),
                  pl.BlockSpec((1, 1, d), lambda bi, i: (bi, 0, 0)), wspec, wspec, wspec],
        out_specs=tile(0),
        out_shape=jax.ShapeDtypeStruct((b, s, d), F32),
        compiler_params=_params("parallel", "parallel"),
        name="merge",
    )(att, cv, proj, proj, x, g1, wa, wc, wo)


N_EXTRACT = PEER_TOPK + 1
N_CAND = sum(N_EXTRACT // (k + 1) for k in range(N_EXTRACT))
N_CAND_ROWS = -(-N_CAND // 8) * 8
SEL_GROUPS = 4


LANES = 128


def _col_max(s):
    parts = [s[r:r + 8] for r in range(0, s.shape[0], 8)]
    while len(parts) > 1:
        parts = [jnp.maximum(parts[i], parts[i + 1]) if i + 1 < len(parts) else parts[i]
                 for i in range(0, len(parts), 2)]
    return jnp.max(parts[0], axis=0, keepdims=True)


def _top_values(s, n):
    out = []
    for r in range(n):
        m = _col_max(s)
        out.append(m)
        s = jnp.where(s == m, -jnp.inf, s)
    return out


def _sort_network(n):
    def merge(lo, hi, r):
        step = r * 2
        if step < hi - lo:
            yield from merge(lo, hi, step)
            yield from merge(lo + r, hi, step)
            yield from [(i, i + r) for i in range(lo + r, hi - r, step)]
        else:
            yield (lo, lo + r)

    def sort(lo, hi):
        if hi - lo >= 1:
            mid = lo + (hi - lo) // 2
            yield from sort(lo, mid)
            yield from sort(mid + 1, hi)
            yield from merge(lo, hi, 1)

    return list(sort(0, n - 1))


def _top_values_tiled(s, n):
    v = [s[r:r + 8] for r in range(0, s.shape[0], 8)]
    for i, j in _sort_network(len(v)):
        v[i], v[j] = jnp.maximum(v[i], v[j]), jnp.minimum(v[i], v[j])
    depth = len(v)
    out = []
    for r in range(n):
        m = jnp.max(v[0], axis=0, keepdims=True)
        out.append(m)
        hit = v[0] == m
        for k in range(min(n - 1 - r, depth)):
            v[k] = jnp.where(hit, v[k + 1] if k + 1 < depth else -jnp.inf, v[k])
    return out


def _lookup_by_value(x, keys, vals, default):
    out = jnp.full(x.shape, default, F32)
    for key, val in zip(keys, vals):
        out = jnp.where(x == key, val, out)
    return out


HI16 = 0xFFFF0000


def _bf16_bits_hi(x):
    return pltpu.bitcast(x.astype(BF16).astype(F32), jnp.uint32) & jnp.uint32(HI16)


def _dup16(x):
    hi = _bf16_bits_hi(x)
    return hi | lax.shift_right_logical(hi, jnp.uint32(16))


def _pack_halves(x):
    hi = _bf16_bits_hi(x)
    r = x.shape[0] // 2
    return lax.shift_right_logical(hi[:r], jnp.uint32(16)) | hi[r:]


def _unpack_halves(w):
    lo = pltpu.bitcast(lax.shift_left(w, jnp.uint32(16)), F32)
    hi = pltpu.bitcast(w & jnp.uint32(HI16), F32)
    return lo, hi


def _peer_sel_kernel(xs_ref, g_ref, sh_ref, sc_ref, wqt_ref, keys_ref,
                     hx_ref, rank_ref, e1_ref, cnt_ref, c_ref, q_ref, s_ref, cand_ref, *, t):
    ht = _rms_modulate(xs_ref[0], g_ref[...], sh_ref[0], sc_ref[0]).T.astype(BF16)
    hx_ref[0] = ht
    q_ref[...] = jnp.dot(wqt_ref[...], ht, preferred_element_type=F32).astype(BF16)
    cand_ref[...] = jnp.full(cand_ref.shape, -jnp.inf, F32)

    def head(hd, carry):
        for p in range(2):
            hp = hd * 2 + p
            q = q_ref[pl.ds(pl.multiple_of(hp * PEER_HALF, PEER_HALF), PEER_HALF), :]
            s_ref[p] = jnp.dot(keys_ref[hp], q, preferred_element_type=F32)

        def group(tg, slot):
            lanes = pl.ds(pl.multiple_of(tg * LANES, LANES), LANES)
            s0 = s_ref[0, :, lanes]
            s1 = s_ref[1, :, lanes]
            a0 = _top_values_tiled(s0, N_EXTRACT)
            a1 = _top_values_tiled(s1, N_EXTRACT)
            r = 0
            for k in range(N_EXTRACT):
                for l in range(N_EXTRACT // (k + 1)):
                    cand_ref[slot, r:r + 1, :] = a0[k] + a1[l]
                    r += 1
            best = _top_values(cand_ref[slot], N_EXTRACT)
            tau = 0.5 * (best[PEER_TOPK - 1] + best[PEER_TOPK])
            z = jnp.ones_like(tau)
            for r in range(1, PEER_TOPK):
                z = z + jnp.exp(best[r] - best[0])
            admitted = []
            for k in range(PEER_TOPK):
                n_k = jnp.zeros_like(tau)
                for l in range(N_EXTRACT // (k + 1)):
                    n_k = n_k + jnp.where(a0[k] + a1[l] >= tau, 1.0, 0.0)
                admitted.append(n_k)
            cnt = _lookup_by_value(s0, a0[:PEER_TOPK], admitted, 0.0)
            rank1 = _lookup_by_value(s1, a1, [float(l) for l in range(N_EXTRACT)], float(N_EXTRACT))
            rank_ref[0, hd, :, lanes] = _pack_halves(rank1)
            e1_ref[0, hd, :, lanes] = _pack_halves(jnp.exp(s1 - a1[0]))
            cnt_ref[0, hd, :, lanes] = _dup16(cnt)
            c_ref[0, hd, :, lanes] = _dup16(jnp.exp(s0 - a0[0]) * (0.5 / z))

        def groups(it, carry2):
            for slot in range(SEL_GROUPS):
                group(it * SEL_GROUPS + slot, slot)
            return carry2

        lax.fori_loop(0, t // (LANES * SEL_GROUPS), groups, 0)
        return carry

    lax.fori_loop(0, PEER_HEADS, head, 0)


def _peer_sel(xs, g, sh, sc, wqt, keys, t):
    b, s, d = xs.shape
    nq = wqt.shape[0]
    sel = pl.BlockSpec((1, PEER_HEADS, PEER_N_KEYS, t), lambda bi, i: (bi, 0, 0, i))
    sel_pk = pl.BlockSpec((1, PEER_HEADS, PEER_N_KEYS // 2, t), lambda bi, i: (bi, 0, 0, i))
    sel_u32 = jax.ShapeDtypeStruct((b, PEER_HEADS, PEER_N_KEYS, s), jnp.uint32)
    pk_u32 = jax.ShapeDtypeStruct((b, PEER_HEADS, PEER_N_KEYS // 2, s), jnp.uint32)
    return pl.pallas_call(
        functools.partial(_peer_sel_kernel, t=t),
        grid=(b, s // t),
        in_specs=[pl.BlockSpec((1, t, d), lambda bi, i: (bi, i, 0)),
                  pl.BlockSpec((1, d), lambda bi, i: (0, 0)),
                  pl.BlockSpec((1, 1, d), lambda bi, i: (bi, 0, 0)),
                  pl.BlockSpec((1, 1, d), lambda bi, i: (bi, 0, 0)),
                  pl.BlockSpec((nq, d), lambda bi, i: (0, 0)),
                  pl.BlockSpec((2 * PEER_HEADS, PEER_N_KEYS, PEER_HALF), lambda bi, i: (0, 0, 0))],
        out_specs=[pl.BlockSpec((1, d, t), lambda bi, i: (bi, 0, i)), sel_pk, sel_pk, sel, sel],
        out_shape=[jax.ShapeDtypeStruct((b, d, s), BF16), pk_u32, pk_u32, sel_u32, sel_u32],
        scratch_shapes=[pltpu.VMEM((nq, t), BF16), pltpu.VMEM((2, PEER_N_KEYS, t), F32),
                        pltpu.VMEM((SEL_GROUPS, N_CAND_ROWS, LANES), F32)],
        compiler_params=_params("parallel", "parallel"),
        name="peer_sel",
    )(xs, g, sh, sc, wqt, keys)


PK_ROWS = 32
W_BLKS = 4
MM_COLS = 256


def _bcast16(row, rows):
    return pltpu.bitcast(jnp.broadcast_to(row, (rows, LANES)), BF16)


def _peer_ffn_kernel(hx_ref, u_ref, vt_ref, vtl_ref, cnt_ref, c_ref, rank_ref, e1_ref, xs_ref, g2_ref, fg_ref,
                     o_ref, acc_ref, act_ref, w_ref, p_ref, *, ib, t):
    ec = pl.program_id(2)
    half = PEER_N_KEYS // 2

    @pl.when(ec == 0)
    def _():
        acc_ref[...] = jnp.zeros_like(acc_ref)
        p_ref[...] = jnp.zeros_like(p_ref)

    def weights(tg, jh, blks):
        lanes = slice(tg * LANES, (tg + 1) * LANES)
        jrows = slice(jh * PK_ROWS, (jh + 1) * PK_ROWS)
        subs = range(0, PK_ROWS, 8)
        w = {(blk, r): jnp.zeros((16, LANES), BF16) for blk in blks for r in subs}
        for hd in range(PEER_HEADS):
            rank_u = rank_ref[0, hd, jrows, lanes]
            e1_u = e1_ref[0, hd, jrows, lanes]
            rank = {r: pltpu.bitcast(rank_u[r:r + 8], BF16) for r in subs}
            e1 = {r: pltpu.bitcast(e1_u[r:r + 8], BF16) for r in subs}
            for blk in blks:
                cnt = _bcast16(cnt_ref[0, hd, blk:blk + 1, lanes], 8)
                c = _bcast16(c_ref[0, hd, blk:blk + 1, lanes], 8)
                for r in subs:
                    w[blk, r] = w[blk, r] + jnp.where(rank[r] < cnt, e1[r], jnp.zeros_like(cnt)) * c
        for blk in blks:
            for r in subs:
                w_ref[blk * half + jh * PK_ROWS + r:blk * half + jh * PK_ROWS + r + 8, lanes] = (
                    pltpu.bitcast(w[blk, r], jnp.uint32))

    units = [(tg, jh, b0) for tg in range(t // LANES) for jh in range(half // PK_ROWS)
             for b0 in range(0, ib, W_BLKS)]
    n_cols = t // MM_COLS
    per = -(-len(units) // (2 * n_cols))
    for n in range(2 * n_cols):
        cols = slice((n // 2) * MM_COLS, (n // 2 + 1) * MM_COLS)
        for tg, jh, b0 in units[n * per:(n + 1) * per]:
            weights(tg, jh, range(b0, b0 + W_BLKS))
        if n % 2 == 0:
            act_ref[:, cols] = jnp.dot(u_ref[...], hx_ref[0, :, cols], preferred_element_type=F32)
        else:
            acc_ref[:, cols] += jnp.dot(vt_ref[...], p_ref[:, cols], preferred_element_type=F32)

    for blk in range(ib):
        w_lo, w_hi = _unpack_halves(w_ref[blk * half:(blk + 1) * half, :])
        for part, wj in enumerate((w_lo, w_hi)):
            rows = slice(blk * PEER_N_KEYS + part * half, blk * PEER_N_KEYS + (part + 1) * half)
            a = act_ref[rows, :]
            p_ref[rows, :] = (wj * a * (1.0 + lax.erf(a * (1.0 / math.sqrt(2.0))))).astype(BF16)

    @pl.when(ec == pl.num_programs(2) - 1)
    def _():
        acc = acc_ref[...] + jnp.dot(vtl_ref[...], p_ref[...], preferred_element_type=F32)
        x = xs_ref[0] + g2_ref[0] * acc.T
        ms = jnp.mean(x * x, axis=-1, keepdims=True)
        o_ref[0] = x * lax.rsqrt(ms + EPS) * fg_ref[...]


def _peer_ffn(hx, u, vt, cnt, c, rank, e1, xs, g2, fg, t, ib):
    b, s, d = xs.shape
    n_exp = u.shape[0]
    ecw = ib * PEER_N_KEYS
    n_chunks = n_exp // ecw
    row_sel = pl.BlockSpec((1, PEER_HEADS, ib, t), lambda bi, i, e: (bi, 0, e, i))
    all_sel = pl.BlockSpec((1, PEER_HEADS, PEER_N_KEYS // 2, t), lambda bi, i, e: (bi, 0, 0, i))
    return pl.pallas_call(
        functools.partial(_peer_ffn_kernel, ib=ib, t=t),
        grid=(b, s // t, n_chunks),
        in_specs=[pl.BlockSpec((1, d, t), lambda bi, i, e: (bi, 0, i)),
                  pl.BlockSpec((ecw, d), lambda bi, i, e: (e, 0)),
                  pl.BlockSpec((d, ecw), lambda bi, i, e: (0, jnp.maximum(e - 1, 0))),
                  pl.BlockSpec((d, ecw), lambda bi, i, e: (0, n_chunks - 1)),
                  row_sel, row_sel, all_sel, all_sel,
                  pl.BlockSpec((1, t, d), lambda bi, i, e: (bi, i, 0)),
                  pl.BlockSpec((1, 1, d), lambda bi, i, e: (bi, 0, 0)),
                  pl.BlockSpec((1, d), lambda bi, i, e: (0, 0))],
        out_specs=pl.BlockSpec((1, t, d), lambda bi, i, e: (bi, i, 0)),
        out_shape=jax.ShapeDtypeStruct((b, s, d), F32),
        scratch_shapes=[pltpu.VMEM((d, t), F32), pltpu.VMEM((ecw, t), F32),
                        pltpu.VMEM((ecw // 2, t), jnp.uint32), pltpu.VMEM((ecw, t), BF16)],
        compiler_params=_params("parallel", "parallel", "arbitrary"),
        name="peer_ffn",
    )(hx, u, vt, vt, cnt, c, rank, e1, xs, g2, fg)


def kernel(x, c, ctx, c_ctx, w_mod, b_mod, norm1_g, norm2_g, w_in, q_norm_g, k_norm_g, w_attn_o, conv_dw,
           conv_b, conv_ln_g, conv_ln_b, w_conv_o, w_out, peer_wq, peer_keys, peer_u, peer_v, final_norm_g):
    b, s, d = x.shape
    n_ctx = ctx.shape[1]
    assert w_mod.shape[0] == 1, "single-layer stack"
    qw = N_Q_HEADS * HEAD_DIM
    kvw = N_KV_HEADS * HEAD_DIM
    assert d == qw and conv_dw.shape[2] == d

    m_pad = -(-(b + 1) // 8) * 8
    cc = jnp.zeros((m_pad, d), F32).at[:b].set(c).at[b].set(c_ctx)
    mod = _mod(cc, w_mod[0], b_mod[0][None])
    sh1, sc1, g1, sh2, sc2, g2 = [mod[:b, None, j * d:(j + 1) * d] for j in range(N_MOD)]
    csh1, csc1 = [jnp.broadcast_to(mod[b:b + 1, None, j * d:(j + 1) * d], (b, 1, d)) for j in range(2)]

    wi = w_in[0]
    o_k, o_v, o_u, o_ga, o_gc = qw, qw + kvw, qw + 2 * kvw, qw + 2 * kvw + 2 * d, qw + 2 * kvw + 3 * d
    w_cat = jnp.concatenate([wi[:, :qw], wi[:, o_u:o_u + 2 * d], wi[:, o_ga:o_gc], wi[:, o_gc:],
                             wi[:, o_k:o_v], wi[:, o_v:o_u]], axis=1).astype(BF16)
    a_blk, b_blk, ga_blk, gc_blk = 1, 2, 3, 4
    kv_col = 5 * d

    n1 = norm1_g[0][None]
    tm = min(512, s)
    proj = _in_proj(x, n1, sh1, sc1, w_cat, tm)
    proj_c = _in_proj(ctx, n1, csh1, csc1, w_cat[:, kv_col:], min(256, n_ctx))

    cos, sin = _rope_tables(s)
    qg, kg = q_norm_g[0][None], k_norm_g[0][None]
    k_lat = _head_prep(proj, kv_col // kvw, N_KV_HEADS, cos, sin, kg, 1.0, tm)
    k_ctx = _head_prep(proj_c, 0, N_KV_HEADS, jnp.ones((n_ctx, HEAD_DIM), F32),
                       jnp.zeros((n_ctx, HEAD_DIM), F32), kg, 1.0, min(256, n_ctx))
    k_all = jnp.concatenate([k_lat, k_ctx], axis=1)
    v_all = jnp.concatenate([proj[:, :, kv_col + kvw:], proj_c[:, :, kvw:]], axis=1)
    v_all = v_all.reshape(b, s + n_ctx, N_KV_HEADS, HEAD_DIM)
    v_ones = jnp.concatenate([v_all, jnp.ones_like(v_all)], axis=-1).reshape(b, s + n_ctx, 2 * kvw)
    att = _attention(proj, cos, sin, qg, k_all, v_ones, min(256, s))

    cv = _conv(proj, a_blk, b_blk, conv_dw[0], conv_b[0][None], conv_ln_g[0][None], conv_ln_b[0][None], tm)
    xs = _merge(att, cv, proj, ga_blk, gc_blk, x, g1, w_attn_o[0].astype(BF16), w_conv_o[0].astype(BF16),
                w_out[0].astype(BF16), tm)

    tp = min(512, s)
    wqt = peer_wq[0].T.astype(BF16)
    keys = peer_keys[0].reshape(2 * PEER_HEADS, PEER_N_KEYS, PEER_HALF).astype(BF16)
    hx, rank, e1, cnt, cw = _peer_sel(xs, norm2_g[0][None], sh2, sc2, wqt, keys, tp)
    u_b = peer_u[0].astype(BF16)
    vt_b = peer_v[0].T.astype(BF16)
    return _peer_ffn(hx, u_b, vt_b, cnt, cw, rank, e1, xs, g2, final_norm_g[None], tp, 8)
```

```python
import functools
import math

import jax
import jax.numpy as jnp
from jax import lax
from jax.experimental import pallas as pl
from jax.experimental.pallas import tpu as pltpu

F32 = jnp.float32
BF16 = jnp.bfloat16

EPS = 1e-6
HEAD_DIM = 128
N_Q_HEADS = 8
N_KV_HEADS = 2
GQA_GROUP = N_Q_HEADS // N_KV_HEADS
GRID_W = 64
ROPE_THETA = 10000.0
ROPE_AXIS_DIM = HEAD_DIM // 2
CONV_WIDTH = 31
CONV_PAD = CONV_WIDTH // 2
CONV_HALO = 16
N_MOD = 6
PEER_HEADS = 8
PEER_N_KEYS = 128
PEER_TOPK = 16
PEER_HALF = 128

VMEM_LIMIT_BYTES = 56 * 1024 * 1024

NT_DIMS = (((1,), (1,)), ((), ()))


def _params(*semantics, flags=None):
    return pltpu.CompilerParams(dimension_semantics=semantics, vmem_limit_bytes=VMEM_LIMIT_BYTES, flags=flags)


def _rms_modulate(x, g, shift, scale):
    ms = jnp.mean(x * x, axis=-1, keepdims=True)
    y = x * lax.rsqrt(ms + EPS) * g
    return y * (1.0 + scale) + shift


def _mod_kernel(c_ref, w_ref, b_ref, o_ref):
    c = c_ref[...]
    a = c * jax.nn.sigmoid(c)
    o_ref[...] = jnp.dot(a, w_ref[...], preferred_element_type=F32,
                         precision=lax.Precision.HIGHEST) + b_ref[...]


def _mod(cc, w, b):
    m, d = cc.shape
    n = w.shape[1]
    tn = n // 4
    return pl.pallas_call(
        _mod_kernel,
        grid=(n // tn,),
        in_specs=[pl.BlockSpec((m, d), lambda j: (0, 0)),
                  pl.BlockSpec((d, tn), lambda j: (0, j)),
                  pl.BlockSpec((1, tn), lambda j: (0, j))],
        out_specs=pl.BlockSpec((m, tn), lambda j: (0, j)),
        out_shape=jax.ShapeDtypeStruct((m, n), F32),
        compiler_params=_params("arbitrary"),
        name="mod",
    )(cc, w, b)


def _in_proj_kernel(x_ref, g_ref, sh_ref, sc_ref, w_ref, o_ref):
    h = _rms_modulate(x_ref[0], g_ref[...], sh_ref[0], sc_ref[0]).astype(BF16)
    o_ref[0] = jnp.dot(h, w_ref[...], preferred_element_type=F32).astype(o_ref.dtype)


def _in_proj(x, g, sh, sc, w, tm):
    b, s, d = x.shape
    n = w.shape[1]
    return pl.pallas_call(
        _in_proj_kernel,
        grid=(b, s // tm),
        in_specs=[pl.BlockSpec((1, tm, d), lambda bi, i: (bi, i, 0)),
                  pl.BlockSpec((1, d), lambda bi, i: (0, 0)),
                  pl.BlockSpec((1, 1, d), lambda bi, i: (bi, 0, 0)),
                  pl.BlockSpec((1, 1, d), lambda bi, i: (bi, 0, 0)),
                  pl.BlockSpec((d, n), lambda bi, i: (0, 0))],
        out_specs=pl.BlockSpec((1, tm, n), lambda bi, i: (bi, i, 0)),
        out_shape=jax.ShapeDtypeStruct((b, s, n), BF16),
        compiler_params=_params("parallel", "parallel"),
        name="in_proj",
    )(x, g, sh, sc, w)


def _norm_rope(x, g, cos, sin):
    ms = jnp.mean(x * x, axis=-1, keepdims=True)
    y = x * lax.rsqrt(ms + EPS) * g
    lane = lax.broadcasted_iota(jnp.int32, y.shape, 1)
    first = (lane % ROPE_AXIS_DIM) < (ROPE_AXIS_DIM // 2)
    partner = jnp.where(first, pltpu.roll(y, HEAD_DIM - ROPE_AXIS_DIM // 2, 1),
                        pltpu.roll(y, ROPE_AXIS_DIM // 2, 1))
    return y * cos + partner * sin


def _head_prep_kernel(x_ref, cos_ref, sin_ref, g_ref, o_ref, *, n_heads, scale):
    cos = cos_ref[...]
    sin = sin_ref[...]
    g = g_ref[...] * scale
    for hd in range(n_heads):
        x = x_ref[0, :, hd * HEAD_DIM:(hd + 1) * HEAD_DIM].astype(F32)
        o_ref[0, :, hd * HEAD_DIM:(hd + 1) * HEAD_DIM] = _norm_rope(x, g, cos, sin).astype(o_ref.dtype)


def _head_prep(x, col_block, n_heads, cos, sin, g, scale, tm):
    b, s, _ = x.shape
    w = n_heads * HEAD_DIM
    return pl.pallas_call(
        functools.partial(_head_prep_kernel, n_heads=n_heads, scale=scale),
        grid=(b, s // tm),
        in_specs=[pl.BlockSpec((1, tm, w), lambda bi, i: (bi, i, col_block)),
                  pl.BlockSpec((tm, HEAD_DIM), lambda bi, i: (i, 0)),
                  pl.BlockSpec((tm, HEAD_DIM), lambda bi, i: (i, 0)),
                  pl.BlockSpec((1, HEAD_DIM), lambda bi, i: (0, 0))],
        out_specs=pl.BlockSpec((1, tm, w), lambda bi, i: (bi, i, 0)),
        out_shape=jax.ShapeDtypeStruct((b, s, w), BF16),
        compiler_params=_params("parallel", "parallel"),
        name="head_prep",
    )(x, cos, sin, g)


def _rope_tables(s):
    t = jnp.arange(s, dtype=jnp.int32)
    row = (t // GRID_W).astype(F32)
    col = (t % GRID_W).astype(F32)
    inv = ROPE_THETA ** (-jnp.arange(0, ROPE_AXIS_DIM, 2, dtype=F32) / ROPE_AXIS_DIM)
    ang_r = row[:, None] * inv
    ang_c = col[:, None] * inv
    cos = jnp.concatenate([jnp.cos(ang_r), jnp.cos(ang_r), jnp.cos(ang_c), jnp.cos(ang_c)], axis=-1)
    sin = jnp.concatenate([-jnp.sin(ang_r), jnp.sin(ang_r), -jnp.sin(ang_c), jnp.sin(ang_c)], axis=-1)
    return cos, sin


def _attention_kernel(q_ref, cos_ref, sin_ref, qg_ref, k_ref, v_ref, o_ref):
    cos = cos_ref[...]
    sin = sin_ref[...]
    qg = qg_ref[...] * (HEAD_DIM ** -0.5 * math.log2(math.e))
    for hd in range(N_Q_HEADS):
        kh = hd // GQA_GROUP
        k = k_ref[0, :, kh * HEAD_DIM:(kh + 1) * HEAD_DIM]
        v1 = v_ref[0, :, 2 * kh * HEAD_DIM:2 * (kh + 1) * HEAD_DIM]
        x = q_ref[0, :, hd * HEAD_DIM:(hd + 1) * HEAD_DIM].astype(F32)
        q = _norm_rope(x, qg, cos, sin).astype(BF16)
        s = lax.dot_general(q, k, NT_DIMS, preferred_element_type=F32)
        p = jnp.exp2(s - jnp.max(s, axis=-1, keepdims=True))
        o = jnp.dot(p.astype(BF16), v1, preferred_element_type=F32)
        o_ref[0, :, hd * HEAD_DIM:(hd + 1) * HEAD_DIM] = (o[:, :HEAD_DIM] / o[:, HEAD_DIM:]).astype(o_ref.dtype)


def _attention(proj, cos, sin, qg, k, v, tq):
    b, s, _ = proj.shape
    skv = k.shape[1]
    qw = N_Q_HEADS * HEAD_DIM
    kvw = N_KV_HEADS * HEAD_DIM
    return pl.pallas_call(
        _attention_kernel,
        grid=(b, s // tq),
        in_specs=[pl.BlockSpec((1, tq, qw), lambda bi, i: (bi, i, 0)),
                  pl.BlockSpec((tq, HEAD_DIM), lambda bi, i: (i, 0)),
                  pl.BlockSpec((tq, HEAD_DIM), lambda bi, i: (i, 0)),
                  pl.BlockSpec((1, HEAD_DIM), lambda bi, i: (0, 0)),
                  pl.BlockSpec((1, skv, kvw), lambda bi, i: (bi, 0, 0)),
                  pl.BlockSpec((1, skv, 2 * kvw), lambda bi, i: (bi, 0, 0))],
        out_specs=pl.BlockSpec((1, tq, qw), lambda bi, i: (bi, i, 0)),
        out_shape=jax.ShapeDtypeStruct((b, s, qw), BF16),
        compiler_params=_params("parallel", "parallel"),
        name="attention",
    )(proj, cos, sin, qg, k, v)


def _conv_kernel(ap_ref, a_ref, an_ref, bp_ref, b_ref, bn_ref, w_ref, cb_ref, lg_ref, lb_ref,
                 o_ref, gext_ref, acc_ref, sh_ref, *, ts, rows, strip):
    i = pl.program_id(1)
    last = pl.num_programs(1) - 1

    def glu(a, b):
        return a.astype(F32) * jax.nn.sigmoid(b.astype(F32))

    gext_ref[0:CONV_HALO, :] = jnp.where(i > 0, glu(ap_ref[0], bp_ref[0]), 0.0)
    gext_ref[CONV_HALO:CONV_HALO + ts, :] = glu(a_ref[0], b_ref[0])
    gext_ref[CONV_HALO + ts:, :] = jnp.where(i < last, glu(an_ref[0], bn_ref[0]), 0.0)

    c = a_ref.shape[-1]
    off = CONV_HALO - CONV_PAD

    def chunk(r, carry):
        r0 = pl.multiple_of(r * rows, rows)
        for cs in range(c // strip):
            lanes = slice(cs * strip, (cs + 1) * strip)
            win = gext_ref[pl.ds(r0, rows + 2 * CONV_HALO), lanes]
            for r in range(8):
                sh_ref[r] = win[r:r + rows + 2 * CONV_HALO - 8, :]
            acc = jnp.zeros((rows, strip), F32)
            for k in range(CONV_WIDTH):
                q, r = divmod(off + k, 8)
                acc = acc + sh_ref[r, 8 * q:8 * q + rows, :] * w_ref[k:k + 1, lanes]
            acc_ref[pl.ds(r0, rows), lanes] = acc
        return carry

    lax.fori_loop(0, ts // rows, chunk, 0)

    y = acc_ref[...] + cb_ref[...]
    mu = jnp.mean(y, axis=-1, keepdims=True)
    yc = y - mu
    var = jnp.mean(yc * yc, axis=-1, keepdims=True)
    z = yc * lax.rsqrt(var + EPS) * lg_ref[...] + lb_ref[...]
    o_ref[0] = (z * jax.nn.sigmoid(z)).astype(o_ref.dtype)


def _conv(proj, a_blk, b_blk, w, cb, lg, lb, ts):
    b, s, _ = proj.shape
    c = w.shape[1]
    hb = ts // CONV_HALO
    n_halo = s // CONV_HALO

    def main(col):
        return pl.BlockSpec((1, ts, c), lambda bi, i: (bi, i, col))

    def prev(col):
        return pl.BlockSpec((1, CONV_HALO, c), lambda bi, i: (bi, jnp.maximum(i * hb - 1, 0), col))

    def nxt(col):
        return pl.BlockSpec((1, CONV_HALO, c), lambda bi, i: (bi, jnp.minimum((i + 1) * hb, n_halo - 1), col))

    vec = pl.BlockSpec((1, c), lambda bi, i: (0, 0))
    rows, strip = 64, 256
    return pl.pallas_call(
        functools.partial(_conv_kernel, ts=ts, rows=rows, strip=strip),
        grid=(b, s // ts),
        in_specs=[prev(a_blk), main(a_blk), nxt(a_blk), prev(b_blk), main(b_blk), nxt(b_blk),
                  pl.BlockSpec((CONV_WIDTH, c), lambda bi, i: (0, 0)), vec, vec, vec],
        out_specs=pl.BlockSpec((1, ts, c), lambda bi, i: (bi, i, 0)),
        out_shape=jax.ShapeDtypeStruct((b, s, c), BF16),
        scratch_shapes=[pltpu.VMEM((ts + 2 * CONV_HALO, c), F32), pltpu.VMEM((ts, c), F32),
                        pltpu.VMEM((8, rows + 2 * CONV_HALO - 8, strip), F32)],
        compiler_params=_params("parallel", "parallel"),
        name="conv",
    )(proj, proj, proj, proj, proj, proj, w, cb, lg, lb)


def _merge_kernel(att_ref, cv_ref, ga_ref, gc_ref, x_ref, g1_ref, wa_ref, wc_ref, wo_ref, o_ref):
    y_att = jnp.dot(att_ref[0], wa_ref[...], preferred_element_type=F32)
    y_conv = jnp.dot(cv_ref[0], wc_ref[...], preferred_element_type=F32)
    merged = (jax.nn.sigmoid(ga_ref[0].astype(F32)) * y_att
              + jax.nn.sigmoid(gc_ref[0].astype(F32)) * y_conv)
    y = jnp.dot(merged.astype(BF16), wo_ref[...], preferred_element_type=F32)
    o_ref[0] = x_ref[0] + g1_ref[0] * y


def _merge(att, cv, proj, ga_blk, gc_blk, x, g1, wa, wc, wo, tm):
    b, s, d = x.shape
    tile = lambda col: pl.BlockSpec((1, tm, d), lambda bi, i: (bi, i, col))
    wspec = pl.BlockSpec((d, d), lambda bi, i: (0, 0))
    return pl.pallas_call(
        _merge_kernel,
        grid=(b, s // tm),
        in_specs=[tile(0), tile(0), tile(ga_blk), tile(gc_blk), tile(0),
                  pl.BlockSpec((1, 1, d), lambda bi, i: (bi, 0, 0)), wspec, wspec, wspec],
        out_specs=tile(0),
        out_shape=jax.ShapeDtypeStruct((b, s, d), F32),
        compiler_params=_params("parallel", "parallel"),
        name="merge",
    )(att, cv, proj, proj, x, g1, wa, wc, wo)


N_EXTRACT = PEER_TOPK + 1
N_CAND = sum(N_EXTRACT // (k + 1) for k in range(N_EXTRACT))
N_CAND_ROWS = -(-N_CAND // 8) * 8
SEL_GROUPS = 4
INV_SQRT2 = 1.0 / math.sqrt(2.0)


LANES = 128


def _col_max(s):
    parts = [s[r:r + 8] for r in range(0, s.shape[0], 8)]
    while len(parts) > 1:
        parts = [jnp.maximum(parts[i], parts[i + 1]) if i + 1 < len(parts) else parts[i]
                 for i in range(0, len(parts), 2)]
    return jnp.max(parts[0], axis=0, keepdims=True)


def _top_values(s, n):
    out = []
    for r in range(n):
        m = _col_max(s)
        out.append(m)
        s = jnp.where(s == m, -jnp.inf, s)
    return out


def _sort_network(n):
    def merge(lo, hi, r):
        step = r * 2
        if step < hi - lo:
            yield from merge(lo, hi, step)
            yield from merge(lo + r, hi, step)
            yield from [(i, i + r) for i in range(lo + r, hi - r, step)]
        else:
            yield (lo, lo + r)

    def sort(lo, hi):
        if hi - lo >= 1:
            mid = lo + (hi - lo) // 2
            yield from sort(lo, mid)
            yield from sort(mid + 1, hi)
            yield from merge(lo, hi, 1)

    return list(sort(0, n - 1))


def _top_values_tiled(s, n):
    v = [s[r:r + 8] for r in range(0, s.shape[0], 8)]
    for i, j in _sort_network(len(v)):
        v[i], v[j] = jnp.maximum(v[i], v[j]), jnp.minimum(v[i], v[j])
    depth = len(v)
    out = []
    for r in range(n):
        m = jnp.max(v[0], axis=0, keepdims=True)
        out.append(m)
        hit = v[0] == m
        for k in range(min(n - 1 - r, depth)):
            v[k] = jnp.where(hit, v[k + 1] if k + 1 < depth else -jnp.inf, v[k])
    return out


def _lookup_by_value(x, keys, vals, default):
    out = jnp.full(x.shape, default, F32)
    for key, val in zip(keys, vals):
        out = jnp.where(x == key, val, out)
    return out


HI16 = 0xFFFF0000


def _bf16_bits_hi(x):
    return pltpu.bitcast(x.astype(BF16).astype(F32), jnp.uint32) & jnp.uint32(HI16)


def _dup16(x):
    hi = _bf16_bits_hi(x)
    return hi | lax.shift_right_logical(hi, jnp.uint32(16))


def _pack_halves(x):
    hi = _bf16_bits_hi(x)
    r = x.shape[0] // 2
    return lax.shift_right_logical(hi[:r], jnp.uint32(16)) | hi[r:]


def _unpack_halves(w):
    lo = pltpu.bitcast(lax.shift_left(w, jnp.uint32(16)), F32)
    hi = pltpu.bitcast(w & jnp.uint32(HI16), F32)
    return lo, hi


def _peer_sel_kernel(xs_ref, g_ref, sh_ref, sc_ref, wqt_ref, keys_ref,
                     hx_ref, rank_ref, e1_ref, cnt_ref, c_ref, q_ref, s_ref, cand_ref, *, t):
    ht = _rms_modulate(xs_ref[0], g_ref[...], sh_ref[0], sc_ref[0]).T
    hx_ref[0] = (ht * INV_SQRT2).astype(BF16)
    q_ref[...] = jnp.dot(wqt_ref[...], ht.astype(BF16), preferred_element_type=F32).astype(BF16)
    cand_ref[...] = jnp.full(cand_ref.shape, -jnp.inf, F32)

    def head(hd, carry):
        for p in range(2):
            hp = hd * 2 + p
            q = q_ref[pl.ds(pl.multiple_of(hp * PEER_HALF, PEER_HALF), PEER_HALF), :]
            s_ref[p] = jnp.dot(keys_ref[hp], q, preferred_element_type=F32)

        def group(tg, slot):
            lanes = pl.ds(pl.multiple_of(tg * LANES, LANES), LANES)
            s0 = s_ref[0, :, lanes]
            s1 = s_ref[1, :, lanes]
            a0 = _top_values_tiled(s0, N_EXTRACT)
            a1 = _top_values_tiled(s1, N_EXTRACT)
            r = 0
            for k in range(N_EXTRACT):
                for l in range(N_EXTRACT // (k + 1)):
                    cand_ref[slot, r:r + 1, :] = a0[k] + a1[l]
                    r += 1
            best = _top_values(cand_ref[slot], N_EXTRACT)
            tau = 0.5 * (best[PEER_TOPK - 1] + best[PEER_TOPK])
            z = jnp.ones_like(tau)
            for r in range(1, PEER_TOPK):
                z = z + jnp.exp(best[r] - best[0])
            admitted = []
            for k in range(PEER_TOPK):
                n_k = jnp.zeros_like(tau)
                for l in range(N_EXTRACT // (k + 1)):
                    n_k = n_k + jnp.where(a0[k] + a1[l] >= tau, 1.0, 0.0)
                admitted.append(n_k)
            cnt = _lookup_by_value(s0, a0[:PEER_TOPK], admitted, 0.0)
            rank1 = _lookup_by_value(s1, a1, [float(l) for l in range(N_EXTRACT)], float(N_EXTRACT))
            rank_ref[0, hd, :, lanes] = _pack_halves(rank1)
            e1_ref[0, hd, :, lanes] = _pack_halves(jnp.exp(s1 - a1[0]))
            cnt_ref[0, hd, :, lanes] = _dup16(cnt)
            c_ref[0, hd, :, lanes] = _dup16(jnp.exp(s0 - a0[0]) * (INV_SQRT2 / z))

        def groups(it, carry2):
            for slot in range(SEL_GROUPS):
                group(it * SEL_GROUPS + slot, slot)
            return carry2

        lax.fori_loop(0, t // (LANES * SEL_GROUPS), groups, 0)
        return carry

    lax.fori_loop(0, PEER_HEADS, head, 0)


def _peer_sel(xs, g, sh, sc, wqt, keys, t):
    b, s, d = xs.shape
    nq = wqt.shape[0]
    sel = pl.BlockSpec((1, PEER_HEADS, PEER_N_KEYS, t), lambda bi, i: (bi, 0, 0, i))
    sel_pk = pl.BlockSpec((1, PEER_HEADS, PEER_N_KEYS // 2, t), lambda bi, i: (bi, 0, 0, i))
    sel_u32 = jax.ShapeDtypeStruct((b, PEER_HEADS, PEER_N_KEYS, s), jnp.uint32)
    pk_u32 = jax.ShapeDtypeStruct((b, PEER_HEADS, PEER_N_KEYS // 2, s), jnp.uint32)
    return pl.pallas_call(
        functools.partial(_peer_sel_kernel, t=t),
        grid=(b, s // t),
        in_specs=[pl.BlockSpec((1, t, d), lambda bi, i: (bi, i, 0)),
                  pl.BlockSpec((1, d), lambda bi, i: (0, 0)),
                  pl.BlockSpec((1, 1, d), lambda bi, i: (bi, 0, 0)),
                  pl.BlockSpec((1, 1, d), lambda bi, i: (bi, 0, 0)),
                  pl.BlockSpec((nq, d), lambda bi, i: (0, 0)),
                  pl.BlockSpec((2 * PEER_HEADS, PEER_N_KEYS, PEER_HALF), lambda bi, i: (0, 0, 0))],
        out_specs=[pl.BlockSpec((1, d, t), lambda bi, i: (bi, 0, i)), sel_pk, sel_pk, sel, sel],
        out_shape=[jax.ShapeDtypeStruct((b, d, s), BF16), pk_u32, pk_u32, sel_u32, sel_u32],
        scratch_shapes=[pltpu.VMEM((nq, t), BF16), pltpu.VMEM((2, PEER_N_KEYS, t), F32),
                        pltpu.VMEM((SEL_GROUPS, N_CAND_ROWS, LANES), F32)],
        compiler_params=_params("parallel", "parallel"),
        name="peer_sel",
    )(xs, g, sh, sc, wqt, keys)


PK_ROWS = 32
W_BLKS = 4
MM_COLS = 256


def _bcast16(row, rows):
    return pltpu.bitcast(jnp.broadcast_to(row, (rows, LANES)), BF16)


def _peer_ffn_kernel(hx_ref, u_ref, vt_ref, vtl_ref, cnt_ref, c_ref, rank_ref, e1_ref, xs_ref, g2_ref, fg_ref,
                     o_ref, acc_ref, act_ref, w_ref, p_ref, *, ib, t):
    ec = pl.program_id(2)
    half = PEER_N_KEYS // 2

    @pl.when(ec == 0)
    def _():
        acc_ref[...] = jnp.zeros_like(acc_ref)
        p_ref[...] = jnp.zeros_like(p_ref)

    def weights(tg, jh, blks):
        lanes = slice(tg * LANES, (tg + 1) * LANES)
        jrows = slice(jh * PK_ROWS, (jh + 1) * PK_ROWS)
        subs = range(0, PK_ROWS, 8)
        w = {(blk, r): jnp.zeros((16, LANES), BF16) for blk in blks for r in subs}
        for hd in range(PEER_HEADS):
            rank_u = rank_ref[0, hd, jrows, lanes]
            e1_u = e1_ref[0, hd, jrows, lanes]
            rank = {r: pltpu.bitcast(rank_u[r:r + 8], BF16) for r in subs}
            e1 = {r: pltpu.bitcast(e1_u[r:r + 8], BF16) for r in subs}
            for blk in blks:
                cnt = _bcast16(cnt_ref[0, hd, blk:blk + 1, lanes], 8)
                c = _bcast16(c_ref[0, hd, blk:blk + 1, lanes], 8)
                for r in subs:
                    w[blk, r] = w[blk, r] + jnp.where(rank[r] < cnt, e1[r], jnp.zeros_like(cnt)) * c
        for blk in blks:
            for r in subs:
                w_ref[blk * half + jh * PK_ROWS + r:blk * half + jh * PK_ROWS + r + 8, lanes] = (
                    pltpu.bitcast(w[blk, r], jnp.uint32))

    units = [(tg, jh, b0) for tg in range(t // LANES) for jh in range(half // PK_ROWS)
             for b0 in range(0, ib, W_BLKS)]
    n_cols = t // MM_COLS
    per = -(-len(units) // (2 * n_cols))
    for n in range(2 * n_cols):
        cols = slice((n // 2) * MM_COLS, (n // 2 + 1) * MM_COLS)
        for tg, jh, b0 in units[n * per:(n + 1) * per]:
            weights(tg, jh, range(b0, b0 + W_BLKS))
        if n % 2 == 0:
            act_ref[:, cols] = jnp.dot(u_ref[...], hx_ref[0, :, cols], preferred_element_type=F32)
        else:
            acc_ref[:, cols] += jnp.dot(vt_ref[...], p_ref[:, cols], preferred_element_type=F32)

    for blk in range(ib):
        w_lo, w_hi = _unpack_halves(w_ref[blk * half:(blk + 1) * half, :])
        for part, wj in enumerate((w_lo, w_hi)):
            rows = slice(blk * PEER_N_KEYS + part * half, blk * PEER_N_KEYS + (part + 1) * half)
            a = act_ref[rows, :]
            p_ref[rows, :] = (wj * a * (1.0 + lax.erf(a))).astype(BF16)

    @pl.when(ec == pl.num_programs(2) - 1)
    def _():
        acc = acc_ref[...] + jnp.dot(vtl_ref[...], p_ref[...], preferred_element_type=F32)
        x = xs_ref[0] + g2_ref[0] * acc.T
        ms = jnp.mean(x * x, axis=-1, keepdims=True)
        o_ref[0] = x * lax.rsqrt(ms + EPS) * fg_ref[...]


def _peer_ffn(hx, u, vt, cnt, c, rank, e1, xs, g2, fg, t, ib):
    b, s, d = xs.shape
    n_exp = u.shape[0]
    ecw = ib * PEER_N_KEYS
    n_chunks = n_exp // ecw
    row_sel = pl.BlockSpec((1, PEER_HEADS, ib, t), lambda bi, i, e: (bi, 0, e, i))
    all_sel = pl.BlockSpec((1, PEER_HEADS, PEER_N_KEYS // 2, t), lambda bi, i, e: (bi, 0, 0, i))
    return pl.pallas_call(
        functools.partial(_peer_ffn_kernel, ib=ib, t=t),
        grid=(b, s // t, n_chunks),
        in_specs=[pl.BlockSpec((1, d, t), lambda bi, i, e: (bi, 0, i)),
                  pl.BlockSpec((ecw, d), lambda bi, i, e: (e, 0)),
                  pl.BlockSpec((d, ecw), lambda bi, i, e: (0, jnp.maximum(e - 1, 0))),
                  pl.BlockSpec((d, ecw), lambda bi, i, e: (0, n_chunks - 1)),
                  row_sel, row_sel, all_sel, all_sel,
                  pl.BlockSpec((1, t, d), lambda bi, i, e: (bi, i, 0)),
                  pl.BlockSpec((1, 1, d), lambda bi, i, e: (bi, 0, 0)),
                  pl.BlockSpec((1, d), lambda bi, i, e: (0, 0))],
        out_specs=pl.BlockSpec((1, t, d), lambda bi, i, e: (bi, i, 0)),
        out_shape=jax.ShapeDtypeStruct((b, s, d), F32),
        scratch_shapes=[pltpu.VMEM((d, t), F32), pltpu.VMEM((ecw, t), F32),
                        pltpu.VMEM((ecw // 2, t), jnp.uint32), pltpu.VMEM((ecw, t), BF16)],
        compiler_params=_params("parallel", "parallel", "arbitrary"),
        name="peer_ffn",
    )(hx, u, vt, vt, cnt, c, rank, e1, xs, g2, fg)


def kernel(x, c, ctx, c_ctx, w_mod, b_mod, norm1_g, norm2_g, w_in, q_norm_g, k_norm_g, w_attn_o, conv_dw,
           conv_b, conv_ln_g, conv_ln_b, w_conv_o, w_out, peer_wq, peer_keys, peer_u, peer_v, final_norm_g):
    b, s, d = x.shape
    n_ctx = ctx.shape[1]
    assert w_mod.shape[0] == 1, "single-layer stack"
    qw = N_Q_HEADS * HEAD_DIM
    kvw = N_KV_HEADS * HEAD_DIM
    assert d == qw and conv_dw.shape[2] == d

    m_pad = -(-(b + 1) // 8) * 8
    cc = jnp.zeros((m_pad, d), F32).at[:b].set(c).at[b].set(c_ctx)
    mod = _mod(cc, w_mod[0], b_mod[0][None])
    sh1, sc1, g1, sh2, sc2, g2 = [mod[:b, None, j * d:(j + 1) * d] for j in range(N_MOD)]
    csh1, csc1 = [jnp.broadcast_to(mod[b:b + 1, None, j * d:(j + 1) * d], (b, 1, d)) for j in range(2)]

    wi = w_in[0]
    o_k, o_v, o_u, o_ga, o_gc = qw, qw + kvw, qw + 2 * kvw, qw + 2 * kvw + 2 * d, qw + 2 * kvw + 3 * d
    w_cat = jnp.concatenate([wi[:, :qw], wi[:, o_u:o_u + 2 * d], wi[:, o_ga:o_gc], wi[:, o_gc:],
                             wi[:, o_k:o_v], wi[:, o_v:o_u]], axis=1).astype(BF16)
    a_blk, b_blk, ga_blk, gc_blk = 1, 2, 3, 4
    kv_col = 5 * d

    n1 = norm1_g[0][None]
    tm = min(512, s)
    proj = _in_proj(x, n1, sh1, sc1, w_cat, tm)
    proj_c = _in_proj(ctx, n1, csh1, csc1, w_cat[:, kv_col:], min(256, n_ctx))

    cos, sin = _rope_tables(s)
    qg, kg = q_norm_g[0][None], k_norm_g[0][None]
    k_lat = _head_prep(proj, kv_col // kvw, N_KV_HEADS, cos, sin, kg, 1.0, tm)
    k_ctx = _head_prep(proj_c, 0, N_KV_HEADS, jnp.ones((n_ctx, HEAD_DIM), F32),
                       jnp.zeros((n_ctx, HEAD_DIM), F32), kg, 1.0, min(256, n_ctx))
    k_all = jnp.concatenate([k_lat, k_ctx], axis=1)
    v_all = jnp.concatenate([proj[:, :, kv_col + kvw:], proj_c[:, :, kvw:]], axis=1)
    ones = jnp.ones((b, s + n_ctx, HEAD_DIM), BF16)
    v_ones = jnp.concatenate(
        [piece for kh in range(N_KV_HEADS) for piece in (v_all[:, :, kh * HEAD_DIM:(kh + 1) * HEAD_DIM], ones)],
        axis=-1)
    att = _attention(proj, cos, sin, qg, k_all, v_ones, min(256, s))

    cv = _conv(proj, a_blk, b_blk, conv_dw[0], conv_b[0][None], conv_ln_g[0][None], conv_ln_b[0][None], tm)
    xs = _merge(att, cv, proj, ga_blk, gc_blk, x, g1, w_attn_o[0].astype(BF16), w_conv_o[0].astype(BF16),
                w_out[0].astype(BF16), tm)

    tp = min(512, s)
    wqt = peer_wq[0].T.astype(BF16)
    keys = peer_keys[0].reshape(2 * PEER_HEADS, PEER_N_KEYS, PEER_HALF).astype(BF16)
    hx, rank, e1, cnt, cw = _peer_sel(xs, norm2_g[0][None], sh2, sc2, wqt, keys, tp)
    u_b = peer_u[0].astype(BF16)
    vt_b = peer_v[0].T.astype(BF16)
    return _peer_ffn(hx, u_b, vt_b, cnt, cw, rank, e1, xs, g2, final_norm_g[None], tp, 8)
```

```python
import functools
import math

import jax
import jax.numpy as jnp
from jax import lax
from jax.experimental import pallas as pl
from jax.experimental.pallas import tpu as pltpu

F32 = jnp.float32
BF16 = jnp.bfloat16

EPS = 1e-6
HEAD_DIM = 128
N_Q_HEADS = 8
N_KV_HEADS = 2
GQA_GROUP = N_Q_HEADS // N_KV_HEADS
GRID_W = 64
ROPE_THETA = 10000.0
ROPE_AXIS_DIM = HEAD_DIM // 2
CONV_WIDTH = 31
CONV_PAD = CONV_WIDTH // 2
CONV_HALO = 16
N_MOD = 6
PEER_HEADS = 8
PEER_N_KEYS = 128
PEER_TOPK = 16
PEER_HALF = 128

VMEM_LIMIT_BYTES = 56 * 1024 * 1024

NT_DIMS = (((1,), (1,)), ((), ()))

ROW_TILE = 512
CTX_ROW_TILE = 256
ATTN_Q_TILE = 256
PEER_TOKEN_TILE = 512
PEER_CHUNK_KEYS = 8
CONV_ROWS, CONV_STRIP = 64, 256


def _params(*semantics):
    return pltpu.CompilerParams(dimension_semantics=semantics, vmem_limit_bytes=VMEM_LIMIT_BYTES)


def _rms_modulate(x, g, shift, scale):
    ms = jnp.mean(x * x, axis=-1, keepdims=True)
    y = x * lax.rsqrt(ms + EPS) * g
    return y * (1.0 + scale) + shift


def _mod_kernel(c_ref, w_ref, b_ref, o_ref):
    c = c_ref[...]
    a = c * jax.nn.sigmoid(c)
    o_ref[...] = jnp.dot(a, w_ref[...], preferred_element_type=F32,
                         precision=lax.Precision.HIGHEST) + b_ref[...]


def _mod(cc, w, b):
    m, d = cc.shape
    n = w.shape[1]
    tn = n // 4
    return pl.pallas_call(
        _mod_kernel,
        grid=(n // tn,),
        in_specs=[pl.BlockSpec((m, d), lambda j: (0, 0)),
                  pl.BlockSpec((d, tn), lambda j: (0, j)),
                  pl.BlockSpec((1, tn), lambda j: (0, j))],
        out_specs=pl.BlockSpec((m, tn), lambda j: (0, j)),
        out_shape=jax.ShapeDtypeStruct((m, n), F32),
        compiler_params=_params("arbitrary"),
        name="mod",
    )(cc, w, b)


def _in_proj_kernel(x_ref, g_ref, sh_ref, sc_ref, w_ref, o_ref):
    h = _rms_modulate(x_ref[0], g_ref[...], sh_ref[0], sc_ref[0]).astype(BF16)
    o_ref[0] = jnp.dot(h, w_ref[...], preferred_element_type=F32).astype(o_ref.dtype)


def _in_proj(x, g, sh, sc, w, tm):
    b, s, d = x.shape
    n = w.shape[1]
    return pl.pallas_call(
        _in_proj_kernel,
        grid=(b, s // tm),
        in_specs=[pl.BlockSpec((1, tm, d), lambda bi, i: (bi, i, 0)),
                  pl.BlockSpec((1, d), lambda bi, i: (0, 0)),
                  pl.BlockSpec((1, 1, d), lambda bi, i: (bi, 0, 0)),
                  pl.BlockSpec((1, 1, d), lambda bi, i: (bi, 0, 0)),
                  pl.BlockSpec((d, n), lambda bi, i: (0, 0))],
        out_specs=pl.BlockSpec((1, tm, n), lambda bi, i: (bi, i, 0)),
        out_shape=jax.ShapeDtypeStruct((b, s, n), BF16),
        compiler_params=_params("parallel", "parallel"),
        name="in_proj",
    )(x, g, sh, sc, w)


def _norm_rope(x, g, cos, sin):
    ms = jnp.mean(x * x, axis=-1, keepdims=True)
    y = x * lax.rsqrt(ms + EPS) * g
    lane = lax.broadcasted_iota(jnp.int32, y.shape, 1)
    first = (lane % ROPE_AXIS_DIM) < (ROPE_AXIS_DIM // 2)
    partner = jnp.where(first, pltpu.roll(y, HEAD_DIM - ROPE_AXIS_DIM // 2, 1),
                        pltpu.roll(y, ROPE_AXIS_DIM // 2, 1))
    return y * cos + partner * sin


def _head_prep_kernel(x_ref, cos_ref, sin_ref, g_ref, o_ref, *, n_heads, scale):
    cos = cos_ref[...]
    sin = sin_ref[...]
    g = g_ref[...] * scale
    for hd in range(n_heads):
        x = x_ref[0, :, hd * HEAD_DIM:(hd + 1) * HEAD_DIM].astype(F32)
        o_ref[0, :, hd * HEAD_DIM:(hd + 1) * HEAD_DIM] = _norm_rope(x, g, cos, sin).astype(o_ref.dtype)


def _head_prep(x, col_block, n_heads, cos, sin, g, scale, tm):
    b, s, _ = x.shape
    w = n_heads * HEAD_DIM
    return pl.pallas_call(
        functools.partial(_head_prep_kernel, n_heads=n_heads, scale=scale),
        grid=(b, s // tm),
        in_specs=[pl.BlockSpec((1, tm, w), lambda bi, i: (bi, i, col_block)),
                  pl.BlockSpec((tm, HEAD_DIM), lambda bi, i: (i, 0)),
                  pl.BlockSpec((tm, HEAD_DIM), lambda bi, i: (i, 0)),
                  pl.BlockSpec((1, HEAD_DIM), lambda bi, i: (0, 0))],
        out_specs=pl.BlockSpec((1, tm, w), lambda bi, i: (bi, i, 0)),
        out_shape=jax.ShapeDtypeStruct((b, s, w), BF16),
        compiler_params=_params("parallel", "parallel"),
        name="head_prep",
    )(x, cos, sin, g)


def _rope_tables(s):
    t = jnp.arange(s, dtype=jnp.int32)
    row = (t // GRID_W).astype(F32)
    col = (t % GRID_W).astype(F32)
    inv = ROPE_THETA ** (-jnp.arange(0, ROPE_AXIS_DIM, 2, dtype=F32) / ROPE_AXIS_DIM)
    ang_r = row[:, None] * inv
    ang_c = col[:, None] * inv
    cos = jnp.concatenate([jnp.cos(ang_r), jnp.cos(ang_r), jnp.cos(ang_c), jnp.cos(ang_c)], axis=-1)
    sin = jnp.concatenate([-jnp.sin(ang_r), jnp.sin(ang_r), -jnp.sin(ang_c), jnp.sin(ang_c)], axis=-1)
    return cos, sin


def _attention_kernel(q_ref, cos_ref, sin_ref, qg_ref, k_ref, v_ref, o_ref):
    cos = cos_ref[...]
    sin = sin_ref[...]
    qg = qg_ref[...] * (HEAD_DIM ** -0.5 * math.log2(math.e))
    for hd in range(N_Q_HEADS):
        kh = hd // GQA_GROUP
        k = k_ref[0, :, kh * HEAD_DIM:(kh + 1) * HEAD_DIM]
        v1 = v_ref[0, :, 2 * kh * HEAD_DIM:2 * (kh + 1) * HEAD_DIM]
        x = q_ref[0, :, hd * HEAD_DIM:(hd + 1) * HEAD_DIM].astype(F32)
        q = _norm_rope(x, qg, cos, sin).astype(BF16)
        s = lax.dot_general(q, k, NT_DIMS, preferred_element_type=F32)
        p = jnp.exp2(s - jnp.max(s, axis=-1, keepdims=True))
        o = jnp.dot(p.astype(BF16), v1, preferred_element_type=F32)
        o_ref[0, :, hd * HEAD_DIM:(hd + 1) * HEAD_DIM] = (o[:, :HEAD_DIM] / o[:, HEAD_DIM:]).astype(o_ref.dtype)


def _attention(proj, cos, sin, qg, k, v, tq):
    b, s, _ = proj.shape
    skv = k.shape[1]
    qw = N_Q_HEADS * HEAD_DIM
    kvw = N_KV_HEADS * HEAD_DIM
    return pl.pallas_call(
        _attention_kernel,
        grid=(b, s // tq),
        in_specs=[pl.BlockSpec((1, tq, qw), lambda bi, i: (bi, i, 0)),
                  pl.BlockSpec((tq, HEAD_DIM), lambda bi, i: (i, 0)),
                  pl.BlockSpec((tq, HEAD_DIM), lambda bi, i: (i, 0)),
                  pl.BlockSpec((1, HEAD_DIM), lambda bi, i: (0, 0)),
                  pl.BlockSpec((1, skv, kvw), lambda bi, i: (bi, 0, 0)),
                  pl.BlockSpec((1, skv, 2 * kvw), lambda bi, i: (bi, 0, 0))],
        out_specs=pl.BlockSpec((1, tq, qw), lambda bi, i: (bi, i, 0)),
        out_shape=jax.ShapeDtypeStruct((b, s, qw), BF16),
        compiler_params=_params("parallel", "parallel"),
        name="attention",
    )(proj, cos, sin, qg, k, v)


def _conv_kernel(ap_ref, a_ref, an_ref, bp_ref, b_ref, bn_ref, w_ref, cb_ref, lg_ref, lb_ref,
                 o_ref, gext_ref, acc_ref, sh_ref, *, ts, rows, strip):
    i = pl.program_id(1)
    last = pl.num_programs(1) - 1

    def glu(a, b):
        return a.astype(F32) * jax.nn.sigmoid(b.astype(F32))

    gext_ref[0:CONV_HALO, :] = jnp.where(i > 0, glu(ap_ref[0], bp_ref[0]), 0.0)
    gext_ref[CONV_HALO:CONV_HALO + ts, :] = glu(a_ref[0], b_ref[0])
    gext_ref[CONV_HALO + ts:, :] = jnp.where(i < last, glu(an_ref[0], bn_ref[0]), 0.0)

    c = a_ref.shape[-1]
    off = CONV_HALO - CONV_PAD

    def chunk(r, carry):
        r0 = pl.multiple_of(r * rows, rows)
        for cs in range(c // strip):
            lanes = slice(cs * strip, (cs + 1) * strip)
            win = gext_ref[pl.ds(r0, rows + 2 * CONV_HALO), lanes]
            for r in range(8):
                sh_ref[r] = win[r:r + rows + 2 * CONV_HALO - 8, :]
            acc = jnp.zeros((rows, strip), F32)
            for k in range(CONV_WIDTH):
                q, r = divmod(off + k, 8)
                acc = acc + sh_ref[r, 8 * q:8 * q + rows, :] * w_ref[k:k + 1, lanes]
            acc_ref[pl.ds(r0, rows), lanes] = acc
        return carry

    lax.fori_loop(0, ts // rows, chunk, 0)

    y = acc_ref[...] + cb_ref[...]
    mu = jnp.mean(y, axis=-1, keepdims=True)
    yc = y - mu
    var = jnp.mean(yc * yc, axis=-1, keepdims=True)
    z = yc * lax.rsqrt(var + EPS) * lg_ref[...] + lb_ref[...]
    o_ref[0] = (z * jax.nn.sigmoid(z)).astype(o_ref.dtype)


def _conv(proj, a_blk, b_blk, w, cb, lg, lb, ts):
    b, s, _ = proj.shape
    c = w.shape[1]
    hb = ts // CONV_HALO
    n_halo = s // CONV_HALO

    def main(col):
        return pl.BlockSpec((1, ts, c), lambda bi, i: (bi, i, col))

    def prev(col):
        return pl.BlockSpec((1, CONV_HALO, c), lambda bi, i: (bi, jnp.maximum(i * hb - 1, 0), col))

    def nxt(col):
        return pl.BlockSpec((1, CONV_HALO, c), lambda bi, i: (bi, jnp.minimum((i + 1) * hb, n_halo - 1), col))

    vec = pl.BlockSpec((1, c), lambda bi, i: (0, 0))
    rows, strip = CONV_ROWS, CONV_STRIP
    return pl.pallas_call(
        functools.partial(_conv_kernel, ts=ts, rows=rows, strip=strip),
        grid=(b, s // ts),
        in_specs=[prev(a_blk), main(a_blk), nxt(a_blk), prev(b_blk), main(b_blk), nxt(b_blk),
                  pl.BlockSpec((CONV_WIDTH, c), lambda bi, i: (0, 0)), vec, vec, vec],
        out_specs=pl.BlockSpec((1, ts, c), lambda bi, i: (bi, i, 0)),
        out_shape=jax.ShapeDtypeStruct((b, s, c), BF16),
        scratch_shapes=[pltpu.VMEM((ts + 2 * CONV_HALO, c), F32), pltpu.VMEM((ts, c), F32),
                        pltpu.VMEM((8, rows + 2 * CONV_HALO - 8, strip), F32)],
        compiler_params=_params("parallel", "parallel"),
        name="conv",
    )(proj, proj, proj, proj, proj, proj, w, cb, lg, lb)


def _merge_kernel(att_ref, cv_ref, ga_ref, gc_ref, x_ref, g1_ref, wa_ref, wc_ref, wo_ref, o_ref):
    y_att = jnp.dot(att_ref[0], wa_ref[...], preferred_element_type=F32)
    y_conv = jnp.dot(cv_ref[0], wc_ref[...], preferred_element_type=F32)
    merged = (jax.nn.sigmoid(ga_ref[0].astype(F32)) * y_att
              + jax.nn.sigmoid(gc_ref[0].astype(F32)) * y_conv)
    y = jnp.dot(merged.astype(BF16), wo_ref[...], preferred_element_type=F32)
    o_ref[0] = x_ref[0] + g1_ref[0] * y


def _merge(att, cv, proj, ga_blk, gc_blk, x, g1, wa, wc, wo, tm):
    b, s, d = x.shape
    tile = lambda col: pl.BlockSpec((1, tm, d), lambda bi, i: (bi, i, col))
    wspec = pl.BlockSpec((d, d), lambda bi, i: (0, 0))
    return pl.pallas_call(
        _merge_kernel,
        grid=(b, s // tm),
        in_specs=[tile(0), tile(0), tile(ga_blk), tile(gc_blk), tile(0),
                  pl.BlockSpec((1, 1, d), lambda bi, i: (bi, 0, 0)), wspec, wspec, wspec],
        out_specs=tile(0),
        out_shape=jax.ShapeDtypeStruct((b, s, d), F32),
        compiler_params=_params("parallel", "parallel"),
        name="merge",
    )(att, cv, proj, proj, x, g1, wa, wc, wo)


N_EXTRACT = PEER_TOPK + 1
N_CAND = sum(N_EXTRACT // (k + 1) for k in range(N_EXTRACT))
N_CAND_ROWS = -(-N_CAND // 8) * 8
SEL_GROUPS = 4
INV_SQRT2 = 1.0 / math.sqrt(2.0)


LANES = 128


def _sort_network(n):
    def merge(lo, hi, r):
        step = r * 2
        if step < hi - lo:
            yield from merge(lo, hi, step)
            yield from merge(lo + r, hi, step)
            yield from [(i, i + r) for i in range(lo + r, hi - r, step)]
        else:
            yield (lo, lo + r)

    def sort(lo, hi):
        if hi - lo >= 1:
            mid = lo + (hi - lo) // 2
            yield from sort(lo, mid)
            yield from sort(mid + 1, hi)
            yield from merge(lo, hi, 1)

    return list(sort(0, n - 1))


def _top_values_tiled(s, n):
    v = [s[r:r + 8] for r in range(0, s.shape[0], 8)]
    v += [jnp.full_like(v[0], -jnp.inf)] * ((1 << (len(v) - 1).bit_length()) - len(v))
    for i, j in _sort_network(len(v)):
        v[i], v[j] = jnp.maximum(v[i], v[j]), jnp.minimum(v[i], v[j])
    depth = len(v)
    out = []
    for r in range(n):
        m = jnp.max(v[0], axis=0, keepdims=True)
        out.append(m)
        hit = v[0] == m
        for k in range(min(n - 1 - r, depth)):
            v[k] = jnp.where(hit, v[k + 1] if k + 1 < depth else -jnp.inf, v[k])
    return out


def _lookup_by_value(x, keys, vals, default):
    out = jnp.full(x.shape, default, F32)
    for key, val in zip(keys, vals):
        out = jnp.where(x == key, val, out)
    return out


HI16 = 0xFFFF0000


def _bf16_bits_hi(x):
    return pltpu.bitcast(x.astype(BF16).astype(F32), jnp.uint32) & jnp.uint32(HI16)


def _dup16(x):
    hi = _bf16_bits_hi(x)
    return hi | lax.shift_right_logical(hi, jnp.uint32(16))


def _pack_halves(x):
    hi = _bf16_bits_hi(x)
    r = x.shape[0] // 2
    return lax.shift_right_logical(hi[:r], jnp.uint32(16)) | hi[r:]


def _unpack_halves(w):
    lo = pltpu.bitcast(lax.shift_left(w, jnp.uint32(16)), F32)
    hi = pltpu.bitcast(w & jnp.uint32(HI16), F32)
    return lo, hi


def _peer_sel_kernel(xs_ref, g_ref, sh_ref, sc_ref, wqt_ref, keys_ref,
                     hx_ref, rank_ref, e1_ref, cnt_ref, c_ref, q_ref, s_ref, cand_ref, *, t):
    ht = _rms_modulate(xs_ref[0], g_ref[...], sh_ref[0], sc_ref[0]).T
    hx_ref[0] = (ht * INV_SQRT2).astype(BF16)
    q_ref[...] = jnp.dot(wqt_ref[...], ht.astype(BF16), preferred_element_type=F32).astype(BF16)
    cand_ref[...] = jnp.full(cand_ref.shape, -jnp.inf, F32)

    def head(hd, carry):
        for p in range(2):
            hp = hd * 2 + p
            q = q_ref[pl.ds(pl.multiple_of(hp * PEER_HALF, PEER_HALF), PEER_HALF), :]
            s_ref[p] = jnp.dot(keys_ref[hp], q, preferred_element_type=F32)

        def group(tg, slot):
            lanes = pl.ds(pl.multiple_of(tg * LANES, LANES), LANES)
            s0 = s_ref[0, :, lanes]
            s1 = s_ref[1, :, lanes]
            a0 = _top_values_tiled(s0, N_EXTRACT)
            a1 = _top_values_tiled(s1, N_EXTRACT)
            r = 0
            for k in range(N_EXTRACT):
                for l in range(N_EXTRACT // (k + 1)):
                    cand_ref[slot, r:r + 1, :] = a0[k] + a1[l]
                    r += 1
            best = _top_values_tiled(cand_ref[slot], N_EXTRACT)
            tau = 0.5 * (best[PEER_TOPK - 1] + best[PEER_TOPK])
            z = jnp.ones_like(tau)
            for r in range(1, PEER_TOPK):
                z = z + jnp.exp(best[r] - best[0])
            above = jnp.where(cand_ref[slot] >= tau, 1.0, 0.0)
            admitted, r = [], 0
            for k in range(PEER_TOPK):
                n_k = N_EXTRACT // (k + 1)
                admitted.append(jnp.sum(above[r:r + n_k], axis=0, keepdims=True))
                r += n_k
            cnt = _lookup_by_value(s0, a0[:PEER_TOPK], admitted, 0.0)
            rank1 = _lookup_by_value(s1, a1, [float(l) for l in range(N_EXTRACT)], float(N_EXTRACT))
            rank_ref[0, hd, :, lanes] = _pack_halves(rank1)
            e1_ref[0, hd, :, lanes] = _pack_halves(jnp.exp(s1 - a1[0]))
            cnt_ref[0, hd, :, lanes] = _dup16(cnt)
            c_ref[0, hd, :, lanes] = _dup16(jnp.exp(s0 - a0[0]) * (INV_SQRT2 / z))

        def groups(it, carry2):
            for slot in range(SEL_GROUPS):
                group(it * SEL_GROUPS + slot, slot)
            return carry2

        lax.fori_loop(0, t // (LANES * SEL_GROUPS), groups, 0)
        return carry

    lax.fori_loop(0, PEER_HEADS, head, 0)


def _peer_sel(xs, g, sh, sc, wqt, keys, t):
    b, s, d = xs.shape
    nq = wqt.shape[0]
    sel = pl.BlockSpec((1, PEER_HEADS, PEER_N_KEYS, t), lambda bi, i: (bi, 0, 0, i))
    sel_pk = pl.BlockSpec((1, PEER_HEADS, PEER_N_KEYS // 2, t), lambda bi, i: (bi, 0, 0, i))
    sel_u32 = jax.ShapeDtypeStruct((b, PEER_HEADS, PEER_N_KEYS, s), jnp.uint32)
    pk_u32 = jax.ShapeDtypeStruct((b, PEER_HEADS, PEER_N_KEYS // 2, s), jnp.uint32)
    return pl.pallas_call(
        functools.partial(_peer_sel_kernel, t=t),
        grid=(b, s // t),
        in_specs=[pl.BlockSpec((1, t, d), lambda bi, i: (bi, i, 0)),
                  pl.BlockSpec((1, d), lambda bi, i: (0, 0)),
                  pl.BlockSpec((1, 1, d), lambda bi, i: (bi, 0, 0)),
                  pl.BlockSpec((1, 1, d), lambda bi, i: (bi, 0, 0)),
                  pl.BlockSpec((nq, d), lambda bi, i: (0, 0)),
                  pl.BlockSpec((2 * PEER_HEADS, PEER_N_KEYS, PEER_HALF), lambda bi, i: (0, 0, 0))],
        out_specs=[pl.BlockSpec((1, d, t), lambda bi, i: (bi, 0, i)), sel_pk, sel_pk, sel, sel],
        out_shape=[jax.ShapeDtypeStruct((b, d, s), BF16), pk_u32, pk_u32, sel_u32, sel_u32],
        scratch_shapes=[pltpu.VMEM((nq, t), BF16), pltpu.VMEM((2, PEER_N_KEYS, t), F32),
                        pltpu.VMEM((SEL_GROUPS, N_CAND_ROWS, LANES), F32)],
        compiler_params=_params("parallel", "parallel"),
        name="peer_sel",
    )(xs, g, sh, sc, wqt, keys)


PK_ROWS = 32
W_BLKS = 4
MM_COLS = 256


def _bcast16(row, rows):
    return pltpu.bitcast(jnp.broadcast_to(row, (rows, LANES)), BF16)


def _peer_ffn_kernel(hx_ref, u_ref, vt_ref, vtl_ref, cnt_ref, c_ref, rank_ref, e1_ref, xs_ref, g2_ref, fg_ref,
                     o_ref, acc_ref, act_ref, w_ref, p_ref, *, ib, t):
    ec = pl.program_id(2)
    half = PEER_N_KEYS // 2

    @pl.when(ec == 0)
    def _():
        acc_ref[...] = jnp.zeros_like(acc_ref)
        p_ref[...] = jnp.zeros_like(p_ref)

    def weights(tg, jh, blks):
        lanes = slice(tg * LANES, (tg + 1) * LANES)
        jrows = slice(jh * PK_ROWS, (jh + 1) * PK_ROWS)
        subs = range(0, PK_ROWS, 8)
        w = {(blk, r): jnp.zeros((16, LANES), BF16) for blk in blks for r in subs}
        for hd in range(PEER_HEADS):
            rank_u = rank_ref[0, hd, jrows, lanes]
            e1_u = e1_ref[0, hd, jrows, lanes]
            rank = {r: pltpu.bitcast(rank_u[r:r + 8], BF16) for r in subs}
            e1 = {r: pltpu.bitcast(e1_u[r:r + 8], BF16) for r in subs}
            for blk in blks:
                cnt = _bcast16(cnt_ref[0, hd, blk:blk + 1, lanes], 8)
                c = _bcast16(c_ref[0, hd, blk:blk + 1, lanes], 8)
                for r in subs:
                    w[blk, r] = w[blk, r] + jnp.where(rank[r] < cnt, e1[r], jnp.zeros_like(cnt)) * c
        for blk in blks:
            for r in subs:
                w_ref[blk * half + jh * PK_ROWS + r:blk * half + jh * PK_ROWS + r + 8, lanes] = (
                    pltpu.bitcast(w[blk, r], jnp.uint32))

    units = [(tg, jh, b0) for tg in range(t // LANES) for jh in range(half // PK_ROWS)
             for b0 in range(0, ib, W_BLKS)]
    n_cols = t // MM_COLS
    per = -(-len(units) // (2 * n_cols))
    for n in range(2 * n_cols):
        cols = slice((n // 2) * MM_COLS, (n // 2 + 1) * MM_COLS)
        for tg, jh, b0 in units[n * per:(n + 1) * per]:
            weights(tg, jh, range(b0, b0 + W_BLKS))
        if n % 2 == 0:
            act_ref[:, cols] = jnp.dot(u_ref[...], hx_ref[0, :, cols], preferred_element_type=F32)
        else:
            acc_ref[:, cols] += jnp.dot(vt_ref[...], p_ref[:, cols], preferred_element_type=F32)

    for blk in range(ib):
        w_lo, w_hi = _unpack_halves(w_ref[blk * half:(blk + 1) * half, :])
        for part, wj in enumerate((w_lo, w_hi)):
            rows = slice(blk * PEER_N_KEYS + part * half, blk * PEER_N_KEYS + (part + 1) * half)
            a = act_ref[rows, :]
            p_ref[rows, :] = (wj * a * (1.0 + lax.erf(a))).astype(BF16)

    @pl.when(ec == pl.num_programs(2) - 1)
    def _():
        acc = acc_ref[...] + jnp.dot(vtl_ref[...], p_ref[...], preferred_element_type=F32)
        x = xs_ref[0] + g2_ref[0] * acc.T
        ms = jnp.mean(x * x, axis=-1, keepdims=True)
        o_ref[0] = x * lax.rsqrt(ms + EPS) * fg_ref[...]


def _peer_ffn(hx, u, vt, cnt, c, rank, e1, xs, g2, fg, t, ib):
    b, s, d = xs.shape
    n_exp = u.shape[0]
    ecw = ib * PEER_N_KEYS
    n_chunks = n_exp // ecw
    row_sel = pl.BlockSpec((1, PEER_HEADS, ib, t), lambda bi, i, e: (bi, 0, e, i))
    all_sel = pl.BlockSpec((1, PEER_HEADS, PEER_N_KEYS // 2, t), lambda bi, i, e: (bi, 0, 0, i))
    return pl.pallas_call(
        functools.partial(_peer_ffn_kernel, ib=ib, t=t),
        grid=(b, s // t, n_chunks),
        in_specs=[pl.BlockSpec((1, d, t), lambda bi, i, e: (bi, 0, i)),
                  pl.BlockSpec((ecw, d), lambda bi, i, e: (e, 0)),
                  pl.BlockSpec((d, ecw), lambda bi, i, e: (0, jnp.maximum(e - 1, 0))),
                  pl.BlockSpec((d, ecw), lambda bi, i, e: (0, n_chunks - 1)),
                  row_sel, row_sel, all_sel, all_sel,
                  pl.BlockSpec((1, t, d), lambda bi, i, e: (bi, i, 0)),
                  pl.BlockSpec((1, 1, d), lambda bi, i, e: (bi, 0, 0)),
                  pl.BlockSpec((1, d), lambda bi, i, e: (0, 0))],
        out_specs=pl.BlockSpec((1, t, d), lambda bi, i, e: (bi, i, 0)),
        out_shape=jax.ShapeDtypeStruct((b, s, d), F32),
        scratch_shapes=[pltpu.VMEM((d, t), F32), pltpu.VMEM((ecw, t), F32),
                        pltpu.VMEM((ecw // 2, t), jnp.uint32), pltpu.VMEM((ecw, t), BF16)],
        compiler_params=_params("parallel", "parallel", "arbitrary"),
        name="peer_ffn",
    )(hx, u, vt, vt, cnt, c, rank, e1, xs, g2, fg)


def kernel(x, c, ctx, c_ctx, w_mod, b_mod, norm1_g, norm2_g, w_in, q_norm_g, k_norm_g, w_attn_o, conv_dw,
           conv_b, conv_ln_g, conv_ln_b, w_conv_o, w_out, peer_wq, peer_keys, peer_u, peer_v, final_norm_g):
    b, s, d = x.shape
    n_ctx = ctx.shape[1]
    assert w_mod.shape[0] == 1, "single-layer stack"
    qw = N_Q_HEADS * HEAD_DIM
    kvw = N_KV_HEADS * HEAD_DIM
    assert d == qw and conv_dw.shape[2] == d

    m_pad = -(-(b + 1) // 8) * 8
    cc = jnp.zeros((m_pad, d), F32).at[:b].set(c).at[b].set(c_ctx)
    mod = _mod(cc, w_mod[0], b_mod[0][None])
    sh1, sc1, g1, sh2, sc2, g2 = [mod[:b, None, j * d:(j + 1) * d] for j in range(N_MOD)]
    csh1, csc1 = [jnp.broadcast_to(mod[b:b + 1, None, j * d:(j + 1) * d], (b, 1, d)) for j in range(2)]

    wi = w_in[0]
    o_k, o_v, o_u, o_ga, o_gc = qw, qw + kvw, qw + 2 * kvw, qw + 2 * kvw + 2 * d, qw + 2 * kvw + 3 * d
    w_cat = jnp.concatenate([wi[:, :qw], wi[:, o_u:o_u + 2 * d], wi[:, o_ga:o_gc], wi[:, o_gc:],
                             wi[:, o_k:o_v], wi[:, o_v:o_u]], axis=1).astype(BF16)
    a_blk, b_blk, ga_blk, gc_blk = 1, 2, 3, 4
    kv_col = 5 * d

    n1 = norm1_g[0][None]
    tm = min(ROW_TILE, s)
    tc = min(CTX_ROW_TILE, n_ctx)
    proj = _in_proj(x, n1, sh1, sc1, w_cat, tm)
    proj_c = _in_proj(ctx, n1, csh1, csc1, w_cat[:, kv_col:], tc)

    cos, sin = _rope_tables(s)
    qg, kg = q_norm_g[0][None], k_norm_g[0][None]
    k_lat = _head_prep(proj, kv_col // kvw, N_KV_HEADS, cos, sin, kg, 1.0, tm)
    k_ctx = _head_prep(proj_c, 0, N_KV_HEADS, jnp.ones((n_ctx, HEAD_DIM), F32),
                       jnp.zeros((n_ctx, HEAD_DIM), F32), kg, 1.0, tc)
    k_all = jnp.concatenate([k_lat, k_ctx], axis=1)
    v_all = jnp.concatenate([proj[:, :, kv_col + kvw:], proj_c[:, :, kvw:]], axis=1)
    ones = jnp.ones((b, s + n_ctx, HEAD_DIM), BF16)
    v_ones = jnp.concatenate(
        [piece for kh in range(N_KV_HEADS) for piece in (v_all[:, :, kh * HEAD_DIM:(kh + 1) * HEAD_DIM], ones)],
        axis=-1)
    att = _attention(proj, cos, sin, qg, k_all, v_ones, min(ATTN_Q_TILE, s))

    cv = _conv(proj, a_blk, b_blk, conv_dw[0], conv_b[0][None], conv_ln_g[0][None], conv_ln_b[0][None], tm)
    xs = _merge(att, cv, proj, ga_blk, gc_blk, x, g1, w_attn_o[0].astype(BF16), w_conv_o[0].astype(BF16),
                w_out[0].astype(BF16), tm)

    tp = min(PEER_TOKEN_TILE, s)
    wqt = peer_wq[0].T.astype(BF16)
    keys = peer_keys[0].reshape(2 * PEER_HEADS, PEER_N_KEYS, PEER_HALF).astype(BF16)
    hx, rank, e1, cnt, cw = _peer_sel(xs, norm2_g[0][None], sh2, sc2, wqt, keys, tp)
    u_b = peer_u[0].astype(BF16)
    vt_b = peer_v[0].T.astype(BF16)
    return _peer_ffn(hx, u_b, vt_b, cnt, cw, rank, e1, xs, g2, final_norm_g[None], tp, PEER_CHUNK_KEYS)
```

```python
import functools
import math

import jax
import jax.numpy as jnp
from jax import lax
from jax.experimental import pallas as pl
from jax.experimental.pallas import tpu as pltpu

F32 = jnp.float32
BF16 = jnp.bfloat16

EPS = 1e-6
HEAD_DIM = 128
N_Q_HEADS = 8
N_KV_HEADS = 2
GQA_GROUP = N_Q_HEADS // N_KV_HEADS
GRID_W = 64
ROPE_THETA = 10000.0
ROPE_AXIS_DIM = HEAD_DIM // 2
CONV_WIDTH = 31
CONV_PAD = CONV_WIDTH // 2
CONV_HALO = 16
N_MOD = 6
PEER_HEADS = 8
PEER_N_KEYS = 128
PEER_TOPK = 16
PEER_HALF = 128

VMEM_LIMIT_BYTES = 56 * 1024 * 1024

NT_DIMS = (((1,), (1,)), ((), ()))

ROW_TILE = 512
CTX_ROW_TILE = 256
ATTN_Q_TILE = 256
PEER_TOKEN_TILE = 512
PEER_CHUNK_KEYS = 8
CONV_ROWS, CONV_STRIP = 64, 256


def _params(*semantics):
    return pltpu.CompilerParams(dimension_semantics=semantics, vmem_limit_bytes=VMEM_LIMIT_BYTES)


def _rms_modulate(x, g, shift, scale):
    ms = jnp.mean(x * x, axis=-1, keepdims=True)
    y = x * lax.rsqrt(ms + EPS) * g
    return y * (1.0 + scale) + shift


def _mod_kernel(c_ref, w_ref, b_ref, o_ref):
    c = c_ref[...]
    a = c * jax.nn.sigmoid(c)
    o_ref[...] = jnp.dot(a, w_ref[...], preferred_element_type=F32,
                         precision=lax.Precision.HIGHEST) + b_ref[...]


def _mod(cc, w, b):
    m, d = cc.shape
    n = w.shape[1]
    tn = n // 4
    return pl.pallas_call(
        _mod_kernel,
        grid=(n // tn,),
        in_specs=[pl.BlockSpec((m, d), lambda j: (0, 0)),
                  pl.BlockSpec((d, tn), lambda j: (0, j)),
                  pl.BlockSpec((1, tn), lambda j: (0, j))],
        out_specs=pl.BlockSpec((m, tn), lambda j: (0, j)),
        out_shape=jax.ShapeDtypeStruct((m, n), F32),
        compiler_params=_params("arbitrary"),
        name="mod",
    )(cc, w, b)


def _in_proj_kernel(x_ref, g_ref, sh_ref, sc_ref, w_ref, o_ref):
    h = _rms_modulate(x_ref[0], g_ref[...], sh_ref[0], sc_ref[0]).astype(BF16)
    o_ref[0] = jnp.dot(h, w_ref[...], preferred_element_type=F32).astype(o_ref.dtype)


def _in_proj(x, g, sh, sc, w, tm):
    b, s, d = x.shape
    n = w.shape[1]
    return pl.pallas_call(
        _in_proj_kernel,
        grid=(b, s // tm),
        in_specs=[pl.BlockSpec((1, tm, d), lambda bi, i: (bi, i, 0)),
                  pl.BlockSpec((1, d), lambda bi, i: (0, 0)),
                  pl.BlockSpec((1, 1, d), lambda bi, i: (bi, 0, 0)),
                  pl.BlockSpec((1, 1, d), lambda bi, i: (bi, 0, 0)),
                  pl.BlockSpec((d, n), lambda bi, i: (0, 0), pipeline_mode=pl.Buffered(1))],
        out_specs=pl.BlockSpec((1, tm, n), lambda bi, i: (bi, i, 0)),
        out_shape=jax.ShapeDtypeStruct((b, s, n), BF16),
        compiler_params=_params("parallel", "parallel"),
        name="in_proj",
    )(x, g, sh, sc, w)


def _norm_rope(x, g, cos, sin):
    ms = jnp.mean(x * x, axis=-1, keepdims=True)
    y = x * lax.rsqrt(ms + EPS) * g
    lane = lax.broadcasted_iota(jnp.int32, y.shape, 1)
    first = (lane % ROPE_AXIS_DIM) < (ROPE_AXIS_DIM // 2)
    partner = jnp.where(first, pltpu.roll(y, HEAD_DIM - ROPE_AXIS_DIM // 2, 1),
                        pltpu.roll(y, ROPE_AXIS_DIM // 2, 1))
    return y * cos + partner * sin


def _head_prep_kernel(x_ref, cos_ref, sin_ref, g_ref, o_ref, *, n_heads, scale):
    cos = cos_ref[...]
    sin = sin_ref[...]
    g = g_ref[...] * scale
    for hd in range(n_heads):
        x = x_ref[0, :, hd * HEAD_DIM:(hd + 1) * HEAD_DIM].astype(F32)
        o_ref[0, :, hd * HEAD_DIM:(hd + 1) * HEAD_DIM] = _norm_rope(x, g, cos, sin).astype(o_ref.dtype)


def _head_prep(x, col_block, n_heads, cos, sin, g, scale, tm):
    b, s, _ = x.shape
    w = n_heads * HEAD_DIM
    return pl.pallas_call(
        functools.partial(_head_prep_kernel, n_heads=n_heads, scale=scale),
        grid=(b, s // tm),
        in_specs=[pl.BlockSpec((1, tm, w), lambda bi, i: (bi, i, col_block)),
                  pl.BlockSpec((tm, HEAD_DIM), lambda bi, i: (i, 0)),
                  pl.BlockSpec((tm, HEAD_DIM), lambda bi, i: (i, 0)),
                  pl.BlockSpec((1, HEAD_DIM), lambda bi, i: (0, 0))],
        out_specs=pl.BlockSpec((1, tm, w), lambda bi, i: (bi, i, 0)),
        out_shape=jax.ShapeDtypeStruct((b, s, w), BF16),
        compiler_params=_params("parallel", "parallel"),
        name="head_prep",
    )(x, cos, sin, g)


def _rope_tables(s):
    t = jnp.arange(s, dtype=jnp.int32)
    row = (t // GRID_W).astype(F32)
    col = (t % GRID_W).astype(F32)
    inv = ROPE_THETA ** (-jnp.arange(0, ROPE_AXIS_DIM, 2, dtype=F32) / ROPE_AXIS_DIM)
    ang_r = row[:, None] * inv
    ang_c = col[:, None] * inv
    cos = jnp.concatenate([jnp.cos(ang_r), jnp.cos(ang_r), jnp.cos(ang_c), jnp.cos(ang_c)], axis=-1)
    sin = jnp.concatenate([-jnp.sin(ang_r), jnp.sin(ang_r), -jnp.sin(ang_c), jnp.sin(ang_c)], axis=-1)
    return cos, sin


def _attention_kernel(q_ref, cos_ref, sin_ref, qg_ref, k_ref, v_ref, o_ref):
    cos = cos_ref[...]
    sin = sin_ref[...]
    qg = qg_ref[...] * (HEAD_DIM ** -0.5 * math.log2(math.e))
    for hd in range(N_Q_HEADS):
        kh = hd // GQA_GROUP
        k = k_ref[0, :, kh * HEAD_DIM:(kh + 1) * HEAD_DIM]
        v1 = v_ref[0, :, 2 * kh * HEAD_DIM:2 * (kh + 1) * HEAD_DIM]
        x = q_ref[0, :, hd * HEAD_DIM:(hd + 1) * HEAD_DIM].astype(F32)
        q = _norm_rope(x, qg, cos, sin).astype(BF16)
        s = lax.dot_general(q, k, NT_DIMS, preferred_element_type=F32)
        p = jnp.exp2(s - jnp.max(s, axis=-1, keepdims=True))
        o = jnp.dot(p.astype(BF16), v1, preferred_element_type=F32)
        o_ref[0, :, hd * HEAD_DIM:(hd + 1) * HEAD_DIM] = (o[:, :HEAD_DIM] / o[:, HEAD_DIM:]).astype(o_ref.dtype)


def _attention(proj, cos, sin, qg, k, v, tq):
    b, s, _ = proj.shape
    skv = k.shape[1]
    qw = N_Q_HEADS * HEAD_DIM
    kvw = N_KV_HEADS * HEAD_DIM
    return pl.pallas_call(
        _attention_kernel,
        grid=(b, s // tq),
        in_specs=[pl.BlockSpec((1, tq, qw), lambda bi, i: (bi, i, 0)),
                  pl.BlockSpec((tq, HEAD_DIM), lambda bi, i: (i, 0)),
                  pl.BlockSpec((tq, HEAD_DIM), lambda bi, i: (i, 0)),
                  pl.BlockSpec((1, HEAD_DIM), lambda bi, i: (0, 0)),
                  pl.BlockSpec((1, skv, kvw), lambda bi, i: (bi, 0, 0)),
                  pl.BlockSpec((1, skv, 2 * kvw), lambda bi, i: (bi, 0, 0))],
        out_specs=pl.BlockSpec((1, tq, qw), lambda bi, i: (bi, i, 0)),
        out_shape=jax.ShapeDtypeStruct((b, s, qw), BF16),
        compiler_params=_params("parallel", "parallel"),
        name="attention",
    )(proj, cos, sin, qg, k, v)


def _conv_kernel(ap_ref, a_ref, an_ref, bp_ref, b_ref, bn_ref, w_ref, cb_ref, lg_ref, lb_ref,
                 o_ref, gext_ref, acc_ref, sh_ref, *, ts, rows, strip):
    i = pl.program_id(1)
    last = pl.num_programs(1) - 1

    def glu(a, b):
        return a.astype(F32) * jax.nn.sigmoid(b.astype(F32))

    gext_ref[0:CONV_HALO, :] = jnp.where(i > 0, glu(ap_ref[0], bp_ref[0]), 0.0)
    gext_ref[CONV_HALO:CONV_HALO + ts, :] = glu(a_ref[0], b_ref[0])
    gext_ref[CONV_HALO + ts:, :] = jnp.where(i < last, glu(an_ref[0], bn_ref[0]), 0.0)

    c = a_ref.shape[-1]
    off = CONV_HALO - CONV_PAD

    def chunk(r, carry):
        r0 = pl.multiple_of(r * rows, rows)
        for cs in range(c // strip):
            lanes = slice(cs * strip, (cs + 1) * strip)
            win = gext_ref[pl.ds(r0, rows + 2 * CONV_HALO), lanes]
            for r in range(8):
                sh_ref[r] = win[r:r + rows + 2 * CONV_HALO - 8, :]
            acc = jnp.zeros((rows, strip), F32)
            for k in range(CONV_WIDTH):
                q, r = divmod(off + k, 8)
                acc = acc + sh_ref[r, 8 * q:8 * q + rows, :] * w_ref[k:k + 1, lanes]
            acc_ref[pl.ds(r0, rows), lanes] = acc
        return carry

    lax.fori_loop(0, ts // rows, chunk, 0)

    y = acc_ref[...] + cb_ref[...]
    mu = jnp.mean(y, axis=-1, keepdims=True)
    yc = y - mu
    var = jnp.mean(yc * yc, axis=-1, keepdims=True)
    z = yc * lax.rsqrt(var + EPS) * lg_ref[...] + lb_ref[...]
    o_ref[0] = (z * jax.nn.sigmoid(z)).astype(o_ref.dtype)


def _conv(proj, a_blk, b_blk, w, cb, lg, lb, ts):
    b, s, _ = proj.shape
    c = w.shape[1]
    hb = ts // CONV_HALO
    n_halo = s // CONV_HALO

    def main(col):
        return pl.BlockSpec((1, ts, c), lambda bi, i: (bi, i, col))

    def prev(col):
        return pl.BlockSpec((1, CONV_HALO, c), lambda bi, i: (bi, jnp.maximum(i * hb - 1, 0), col))

    def nxt(col):
        return pl.BlockSpec((1, CONV_HALO, c), lambda bi, i: (bi, jnp.minimum((i + 1) * hb, n_halo - 1), col))

    vec = pl.BlockSpec((1, c), lambda bi, i: (0, 0))
    rows, strip = CONV_ROWS, CONV_STRIP
    return pl.pallas_call(
        functools.partial(_conv_kernel, ts=ts, rows=rows, strip=strip),
        grid=(b, s // ts),
        in_specs=[prev(a_blk), main(a_blk), nxt(a_blk), prev(b_blk), main(b_blk), nxt(b_blk),
                  pl.BlockSpec((CONV_WIDTH, c), lambda bi, i: (0, 0)), vec, vec, vec],
        out_specs=pl.BlockSpec((1, ts, c), lambda bi, i: (bi, i, 0)),
        out_shape=jax.ShapeDtypeStruct((b, s, c), BF16),
        scratch_shapes=[pltpu.VMEM((ts + 2 * CONV_HALO, c), F32), pltpu.VMEM((ts, c), F32),
                        pltpu.VMEM((8, rows + 2 * CONV_HALO - 8, strip), F32)],
        compiler_params=_params("parallel", "parallel"),
        name="conv",
    )(proj, proj, proj, proj, proj, proj, w, cb, lg, lb)


def _merge_kernel(att_ref, cv_ref, ga_ref, gc_ref, x_ref, g1_ref, wa_ref, wc_ref, wo_ref, o_ref):
    y_att = jnp.dot(att_ref[0], wa_ref[...], preferred_element_type=F32)
    y_conv = jnp.dot(cv_ref[0], wc_ref[...], preferred_element_type=F32)
    merged = (jax.nn.sigmoid(ga_ref[0].astype(F32)) * y_att
              + jax.nn.sigmoid(gc_ref[0].astype(F32)) * y_conv)
    y = jnp.dot(merged.astype(BF16), wo_ref[...], preferred_element_type=F32)
    o_ref[0] = x_ref[0] + g1_ref[0] * y


def _merge(att, cv, proj, ga_blk, gc_blk, x, g1, wa, wc, wo, tm):
    b, s, d = x.shape
    tile = lambda col: pl.BlockSpec((1, tm, d), lambda bi, i: (bi, i, col))
    wspec = pl.BlockSpec((d, d), lambda bi, i: (0, 0), pipeline_mode=pl.Buffered(1))
    return pl.pallas_call(
        _merge_kernel,
        grid=(b, s // tm),
        in_specs=[tile(0), tile(0), tile(ga_blk), tile(gc_blk), tile(0),
                  pl.BlockSpec((1, 1, d), lambda bi, i: (bi, 0, 0)), wspec, wspec, wspec],
        out_specs=tile(0),
        out_shape=jax.ShapeDtypeStruct((b, s, d), F32),
        compiler_params=_params("parallel", "parallel"),
        name="merge",
    )(att, cv, proj, proj, x, g1, wa, wc, wo)


N_EXTRACT = PEER_TOPK + 1
N_CAND = sum(N_EXTRACT // (k + 1) for k in range(N_EXTRACT))
N_CAND_ROWS = -(-N_CAND // 8) * 8
SEL_GROUPS = 4
INV_SQRT2 = 1.0 / math.sqrt(2.0)


LANES = 128


def _sort_network(n):
    def merge(lo, hi, r):
        step = r * 2
        if step < hi - lo:
            yield from merge(lo, hi, step)
            yield from merge(lo + r, hi, step)
            yield from [(i, i + r) for i in range(lo + r, hi - r, step)]
        else:
            yield (lo, lo + r)

    def sort(lo, hi):
        if hi - lo >= 1:
            mid = lo + (hi - lo) // 2
            yield from sort(lo, mid)
            yield from sort(mid + 1, hi)
            yield from merge(lo, hi, 1)

    return list(sort(0, n - 1))


def _top_values_tiled(s, n):
    v = [s[r:r + 8] for r in range(0, s.shape[0], 8)]
    v += [jnp.full_like(v[0], -jnp.inf)] * ((1 << (len(v) - 1).bit_length()) - len(v))
    for i, j in _sort_network(len(v)):
        v[i], v[j] = jnp.maximum(v[i], v[j]), jnp.minimum(v[i], v[j])
    depth = len(v)
    out = []
    for r in range(n):
        m = jnp.max(v[0], axis=0, keepdims=True)
        out.append(m)
        hit = v[0] == m
        for k in range(min(n - 1 - r, depth)):
            v[k] = jnp.where(hit, v[k + 1] if k + 1 < depth else -jnp.inf, v[k])
    return out


def _lookup_by_value(x, keys, vals, default):
    out = jnp.full(x.shape, default, F32)
    for key, val in zip(keys, vals):
        out = jnp.where(x == key, val, out)
    return out


HI16 = 0xFFFF0000


def _bf16_bits_hi(x):
    return pltpu.bitcast(x.astype(BF16).astype(F32), jnp.uint32) & jnp.uint32(HI16)


def _dup16(x):
    hi = _bf16_bits_hi(x)
    return hi | lax.shift_right_logical(hi, jnp.uint32(16))


def _pack_halves(x):
    hi = _bf16_bits_hi(x)
    r = x.shape[0] // 2
    return lax.shift_right_logical(hi[:r], jnp.uint32(16)) | hi[r:]


def _unpack_halves(w):
    lo = pltpu.bitcast(lax.shift_left(w, jnp.uint32(16)), F32)
    hi = pltpu.bitcast(w & jnp.uint32(HI16), F32)
    return lo, hi


def _peer_sel_kernel(xs_ref, g_ref, sh_ref, sc_ref, wqt_ref, keys_ref,
                     hx_ref, rank_ref, e1_ref, cnt_ref, c_ref, q_ref, s_ref, cand_ref, *, t):
    ht = _rms_modulate(xs_ref[0], g_ref[...], sh_ref[0], sc_ref[0]).T
    hx_ref[0] = (ht * INV_SQRT2).astype(BF16)
    q_ref[...] = jnp.dot(wqt_ref[...], ht.astype(BF16), preferred_element_type=F32).astype(BF16)
    cand_ref[...] = jnp.full(cand_ref.shape, -jnp.inf, F32)

    def head(hd, carry):
        for p in range(2):
            hp = hd * 2 + p
            q = q_ref[pl.ds(pl.multiple_of(hp * PEER_HALF, PEER_HALF), PEER_HALF), :]
            s_ref[p] = jnp.dot(keys_ref[hp], q, preferred_element_type=F32)

        def group(tg, slot):
            lanes = pl.ds(pl.multiple_of(tg * LANES, LANES), LANES)
            s0 = s_ref[0, :, lanes]
            s1 = s_ref[1, :, lanes]
            a0 = _top_values_tiled(s0, N_EXTRACT)
            a1 = _top_values_tiled(s1, N_EXTRACT)
            r = 0
            for k in range(N_EXTRACT):
                for l in range(N_EXTRACT // (k + 1)):
                    cand_ref[slot, r:r + 1, :] = a0[k] + a1[l]
                    r += 1
            best = _top_values_tiled(cand_ref[slot], N_EXTRACT)
            tau = 0.5 * (best[PEER_TOPK - 1] + best[PEER_TOPK])
            z = jnp.ones_like(tau)
            for r in range(1, PEER_TOPK):
                z = z + jnp.exp(best[r] - best[0])
            above = jnp.where(cand_ref[slot] >= tau, 1.0, 0.0)
            admitted, r = [], 0
            for k in range(PEER_TOPK):
                n_k = N_EXTRACT // (k + 1)
                admitted.append(jnp.sum(above[r:r + n_k], axis=0, keepdims=True))
                r += n_k
            cnt = _lookup_by_value(s0, a0[:PEER_TOPK], admitted, 0.0)
            rank1 = _lookup_by_value(s1, a1, [float(l) for l in range(N_EXTRACT)], float(N_EXTRACT))
            rank_ref[0, hd, :, lanes] = _pack_halves(rank1)
            e1_ref[0, hd, :, lanes] = _pack_halves(jnp.exp(s1 - a1[0]))
            cnt_ref[0, hd, :, lanes] = _dup16(cnt)
            c_ref[0, hd, :, lanes] = _dup16(jnp.exp(s0 - a0[0]) * (INV_SQRT2 / z))

        def groups(it, carry2):
            for slot in range(SEL_GROUPS):
                group(it * SEL_GROUPS + slot, slot)
            return carry2

        lax.fori_loop(0, t // (LANES * SEL_GROUPS), groups, 0)
        return carry

    lax.fori_loop(0, PEER_HEADS, head, 0)


def _peer_sel(xs, g, sh, sc, wqt, keys, t):
    b, s, d = xs.shape
    nq = wqt.shape[0]
    sel = pl.BlockSpec((1, PEER_HEADS, PEER_N_KEYS, t), lambda bi, i: (bi, 0, 0, i))
    sel_pk = pl.BlockSpec((1, PEER_HEADS, PEER_N_KEYS // 2, t), lambda bi, i: (bi, 0, 0, i))
    sel_u32 = jax.ShapeDtypeStruct((b, PEER_HEADS, PEER_N_KEYS, s), jnp.uint32)
    pk_u32 = jax.ShapeDtypeStruct((b, PEER_HEADS, PEER_N_KEYS // 2, s), jnp.uint32)
    return pl.pallas_call(
        functools.partial(_peer_sel_kernel, t=t),
        grid=(b, s // t),
        in_specs=[pl.BlockSpec((1, t, d), lambda bi, i: (bi, i, 0)),
                  pl.BlockSpec((1, d), lambda bi, i: (0, 0)),
                  pl.BlockSpec((1, 1, d), lambda bi, i: (bi, 0, 0)),
                  pl.BlockSpec((1, 1, d), lambda bi, i: (bi, 0, 0)),
                  pl.BlockSpec((nq, d), lambda bi, i: (0, 0), pipeline_mode=pl.Buffered(1)),
                  pl.BlockSpec((2 * PEER_HEADS, PEER_N_KEYS, PEER_HALF), lambda bi, i: (0, 0, 0))],
        out_specs=[pl.BlockSpec((1, d, t), lambda bi, i: (bi, 0, i)), sel_pk, sel_pk, sel, sel],
        out_shape=[jax.ShapeDtypeStruct((b, d, s), BF16), pk_u32, pk_u32, sel_u32, sel_u32],
        scratch_shapes=[pltpu.VMEM((nq, t), BF16), pltpu.VMEM((2, PEER_N_KEYS, t), F32),
                        pltpu.VMEM((SEL_GROUPS, N_CAND_ROWS, LANES), F32)],
        compiler_params=_params("parallel", "parallel"),
        name="peer_sel",
    )(xs, g, sh, sc, wqt, keys)


PK_ROWS = 32
W_BLKS = 4
MM_COLS = 256


def _bcast16(row, rows):
    return pltpu.bitcast(jnp.broadcast_to(row, (rows, LANES)), BF16)


def _peer_ffn_kernel(hx_ref, u_ref, vt_ref, vtl_ref, cnt_ref, c_ref, rank_ref, e1_ref, xs_ref, g2_ref, fg_ref,
                     o_ref, acc_ref, act_ref, w_ref, p_ref, *, ib, t):
    ec = pl.program_id(2)
    half = PEER_N_KEYS // 2

    @pl.when(ec == 0)
    def _():
        acc_ref[...] = jnp.zeros_like(acc_ref)
        p_ref[...] = jnp.zeros_like(p_ref)

    def weights(tg, jh, blks):
        lanes = slice(tg * LANES, (tg + 1) * LANES)
        jrows = slice(jh * PK_ROWS, (jh + 1) * PK_ROWS)
        subs = range(0, PK_ROWS, 8)
        w = {(blk, r): jnp.zeros((16, LANES), BF16) for blk in blks for r in subs}
        for hd in range(PEER_HEADS):
            rank_u = rank_ref[0, hd, jrows, lanes]
            e1_u = e1_ref[0, hd, jrows, lanes]
            rank = {r: pltpu.bitcast(rank_u[r:r + 8], BF16) for r in subs}
            e1 = {r: pltpu.bitcast(e1_u[r:r + 8], BF16) for r in subs}
            for blk in blks:
                cnt = _bcast16(cnt_ref[0, hd, blk:blk + 1, lanes], 8)
                c = _bcast16(c_ref[0, hd, blk:blk + 1, lanes], 8)
                for r in subs:
                    w[blk, r] = w[blk, r] + jnp.where(rank[r] < cnt, e1[r], jnp.zeros_like(cnt)) * c
        for blk in blks:
            for r in subs:
                w_ref[blk * half + jh * PK_ROWS + r:blk * half + jh * PK_ROWS + r + 8, lanes] = (
                    pltpu.bitcast(w[blk, r], jnp.uint32))

    units = [(tg, jh, b0) for tg in range(t // LANES) for jh in range(half // PK_ROWS)
             for b0 in range(0, ib, W_BLKS)]
    n_cols = t // MM_COLS
    per = -(-len(units) // (2 * n_cols))
    for n in range(2 * n_cols):
        cols = slice((n // 2) * MM_COLS, (n // 2 + 1) * MM_COLS)
        for tg, jh, b0 in units[n * per:(n + 1) * per]:
            weights(tg, jh, range(b0, b0 + W_BLKS))
        if n % 2 == 0:
            act_ref[:, cols] = jnp.dot(u_ref[...], hx_ref[0, :, cols], preferred_element_type=F32)
        else:
            acc_ref[:, cols] += jnp.dot(vt_ref[...], p_ref[:, cols], preferred_element_type=F32)

    for blk in range(ib):
        w_lo, w_hi = _unpack_halves(w_ref[blk * half:(blk + 1) * half, :])
        for part, wj in enumerate((w_lo, w_hi)):
            rows = slice(blk * PEER_N_KEYS + part * half, blk * PEER_N_KEYS + (part + 1) * half)
            a = act_ref[rows, :]
            p_ref[rows, :] = (wj * a * (1.0 + lax.erf(a))).astype(BF16)

    @pl.when(ec == pl.num_programs(2) - 1)
    def _():
        acc = acc_ref[...] + jnp.dot(vtl_ref[...], p_ref[...], preferred_element_type=F32)
        x = xs_ref[0] + g2_ref[0] * acc.T
        ms = jnp.mean(x * x, axis=-1, keepdims=True)
        o_ref[0] = x * lax.rsqrt(ms + EPS) * fg_ref[...]


def _peer_ffn(hx, u, vt, cnt, c, rank, e1, xs, g2, fg, t, ib):
    b, s, d = xs.shape
    n_exp = u.shape[0]
    ecw = ib * PEER_N_KEYS
    n_chunks = n_exp // ecw
    row_sel = pl.BlockSpec((1, PEER_HEADS, ib, t), lambda bi, i, e: (bi, 0, e, i))
    all_sel = pl.BlockSpec((1, PEER_HEADS, PEER_N_KEYS // 2, t), lambda bi, i, e: (bi, 0, 0, i))
    return pl.pallas_call(
        functools.partial(_peer_ffn_kernel, ib=ib, t=t),
        grid=(b, s // t, n_chunks),
        in_specs=[pl.BlockSpec((1, d, t), lambda bi, i, e: (bi, 0, i)),
                  pl.BlockSpec((ecw, d), lambda bi, i, e: (e, 0)),
                  pl.BlockSpec((d, ecw), lambda bi, i, e: (0, jnp.maximum(e - 1, 0))),
                  pl.BlockSpec((d, ecw), lambda bi, i, e: (0, n_chunks - 1), pipeline_mode=pl.Buffered(1)),
                  row_sel, row_sel, all_sel, all_sel,
                  pl.BlockSpec((1, t, d), lambda bi, i, e: (bi, i, 0)),
                  pl.BlockSpec((1, 1, d), lambda bi, i, e: (bi, 0, 0)),
                  pl.BlockSpec((1, d), lambda bi, i, e: (0, 0))],
        out_specs=pl.BlockSpec((1, t, d), lambda bi, i, e: (bi, i, 0)),
        out_shape=jax.ShapeDtypeStruct((b, s, d), F32),
        scratch_shapes=[pltpu.VMEM((d, t), F32), pltpu.VMEM((ecw, t), F32),
                        pltpu.VMEM((ecw // 2, t), jnp.uint32), pltpu.VMEM((ecw, t), BF16)],
        compiler_params=_params("parallel", "parallel", "arbitrary"),
        name="peer_ffn",
    )(hx, u, vt, vt, cnt, c, rank, e1, xs, g2, fg)


def kernel(x, c, ctx, c_ctx, w_mod, b_mod, norm1_g, norm2_g, w_in, q_norm_g, k_norm_g, w_attn_o, conv_dw,
           conv_b, conv_ln_g, conv_ln_b, w_conv_o, w_out, peer_wq, peer_keys, peer_u, peer_v, final_norm_g):
    b, s, d = x.shape
    n_ctx = ctx.shape[1]
    assert w_mod.shape[0] == 1, "single-layer stack"
    qw = N_Q_HEADS * HEAD_DIM
    kvw = N_KV_HEADS * HEAD_DIM
    assert d == qw and conv_dw.shape[2] == d

    m_pad = -(-(b + 1) // 8) * 8
    cc = jnp.zeros((m_pad, d), F32).at[:b].set(c).at[b].set(c_ctx)
    mod = _mod(cc, w_mod[0], b_mod[0][None])
    sh1, sc1, g1, sh2, sc2, g2 = [mod[:b, None, j * d:(j + 1) * d] for j in range(N_MOD)]
    csh1, csc1 = [jnp.broadcast_to(mod[b:b + 1, None, j * d:(j + 1) * d], (b, 1, d)) for j in range(2)]

    wi = w_in[0]
    o_k, o_v, o_u, o_ga, o_gc = qw, qw + kvw, qw + 2 * kvw, qw + 2 * kvw + 2 * d, qw + 2 * kvw + 3 * d
    w_cat = jnp.concatenate([wi[:, :qw], wi[:, o_u:o_u + 2 * d], wi[:, o_ga:o_gc], wi[:, o_gc:],
                             wi[:, o_k:o_v], wi[:, o_v:o_u]], axis=1).astype(BF16)
    a_blk, b_blk, ga_blk, gc_blk = 1, 2, 3, 4
    kv_col = 5 * d

    n1 = norm1_g[0][None]
    tm = min(ROW_TILE, s)
    tc = min(CTX_ROW_TILE, n_ctx)
    proj = _in_proj(x, n1, sh1, sc1, w_cat, tm)
    proj_c = _in_proj(ctx, n1, csh1, csc1, w_cat[:, kv_col:], tc)

    cos, sin = _rope_tables(s)
    qg, kg = q_norm_g[0][None], k_norm_g[0][None]
    k_lat = _head_prep(proj, kv_col // kvw, N_KV_HEADS, cos, sin, kg, 1.0, tm)
    k_ctx = _head_prep(proj_c, 0, N_KV_HEADS, jnp.ones((n_ctx, HEAD_DIM), F32),
                       jnp.zeros((n_ctx, HEAD_DIM), F32), kg, 1.0, tc)
    k_all = jnp.concatenate([k_lat, k_ctx], axis=1)
    v_all = jnp.concatenate([proj[:, :, kv_col + kvw:], proj_c[:, :, kvw:]], axis=1)
    ones = jnp.ones((b, s + n_ctx, HEAD_DIM), BF16)
    v_ones = jnp.concatenate(
        [piece for kh in range(N_KV_HEADS) for piece in (v_all[:, :, kh * HEAD_DIM:(kh + 1) * HEAD_DIM], ones)],
        axis=-1)
    att = _attention(proj, cos, sin, qg, k_all, v_ones, min(ATTN_Q_TILE, s))

    cv = _conv(proj, a_blk, b_blk, conv_dw[0], conv_b[0][None], conv_ln_g[0][None], conv_ln_b[0][None], tm)
    xs = _merge(att, cv, proj, ga_blk, gc_blk, x, g1, w_attn_o[0].astype(BF16), w_conv_o[0].astype(BF16),
                w_out[0].astype(BF16), tm)

    tp = min(PEER_TOKEN_TILE, s)
    wqt = peer_wq[0].T.astype(BF16)
    keys = peer_keys[0].reshape(2 * PEER_HEADS, PEER_N_KEYS, PEER_HALF).astype(BF16)
    hx, rank, e1, cnt, cw = _peer_sel(xs, norm2_g[0][None], sh2, sc2, wqt, keys, tp)
    u_b = peer_u[0].astype(BF16)
    vt_b = peer_v[0].T.astype(BF16)
    return _peer_ffn(hx, u_b, vt_b, cnt, cw, rank, e1, xs, g2, final_norm_g[None], tp, PEER_CHUNK_KEYS)
```
